```python
import jax, jax.numpy as jnp
from jax import lax
import numpy as np

D_MODEL = 2048
BATCH = 8
SEQ = 2048
DEPTH = 2

D_MIX = D_MODEL
HEAD_DIM = 64
ATTN_WIDTH = D_MIX // 2
N_Q_HEADS = ATTN_WIDTH // HEAD_DIM
N_KV_HEADS = N_Q_HEADS // 4
KV_WIDTH = N_KV_HEADS * HEAD_DIM
WINDOW = 128
CONV_WIDTH = D_MIX // 4
CONV_KERNEL = 31
SGU_WIDTH = D_MIX // 4
SGU_HEADS = SGU_WIDTH // HEAD_DIM
CHUNK = 128
D_FF = 4 * D_MODEL
EPS = 1e-6
NEG_INF = -1e30
SPLITS = (ATTN_WIDTH,
          ATTN_WIDTH + KV_WIDTH,
          ATTN_WIDTH + 2 * KV_WIDTH,
          ATTN_WIDTH + 2 * KV_WIDTH + 2 * CONV_WIDTH)
D_IN = ATTN_WIDTH + 2 * KV_WIDTH + 2 * CONV_WIDTH + 2 * SGU_WIDTH

kernel_name = "hymba_conv_sgu_swa_hybrid"


def rms_norm(x, g):
    xf = x.astype(jnp.float32)
    y = xf * lax.rsqrt(jnp.mean(xf * xf, axis=-1, keepdims=True) + EPS)
    return (y * g.astype(jnp.float32)).astype(x.dtype)


def layer_norm(x, g, b):
    xf = x.astype(jnp.float32)
    mu = jnp.mean(xf, axis=-1, keepdims=True)
    xc = xf - mu
    y = xc * lax.rsqrt(jnp.mean(xc * xc, axis=-1, keepdims=True) + EPS)
    return (y * g.astype(jnp.float32) + b.astype(jnp.float32)).astype(x.dtype)


def sliding_window_attention(q, k, v, sinks):
    B, S = q.shape[0], q.shape[1]
    nb = S // WINDOW
    G = N_Q_HEADS // N_KV_HEADS
    qb = q.reshape(B, nb, WINDOW, N_KV_HEADS, G, HEAD_DIM)

    def with_prev(t):
        tb = t.reshape(B, nb, WINDOW, N_KV_HEADS, HEAD_DIM)
        prev = jnp.pad(tb, ((0, 0), (1, 0), (0, 0), (0, 0), (0, 0)))[:, :-1]
        return jnp.concatenate([prev, tb], axis=2)

    kb, vb = with_prev(k), with_prev(v)
    scale = HEAD_DIM ** -0.5
    logits = jnp.einsum('bnqkgd,bnskd->bnkgqs', qb, kb).astype(jnp.float32) * scale
    qi = jnp.arange(WINDOW)[None, :, None]
    sj = jnp.arange(2 * WINDOW)[None, None, :]
    blk = jnp.arange(nb)[:, None, None]
    rel = qi + WINDOW - sj
    key_pos = blk * WINDOW - WINDOW + sj
    mask = (rel >= 0) & (rel < WINDOW) & (key_pos >= 0)
    logits = jnp.where(mask[None, :, None, None], logits, NEG_INF)
    sink = sinks.astype(jnp.float32).reshape(N_KV_HEADS, G)[None, None, :, :, None, None]
    m = jnp.maximum(jnp.max(logits, axis=-1, keepdims=True), sink)
    p = jnp.exp(logits - m)
    denom = jnp.sum(p, axis=-1, keepdims=True) + jnp.exp(sink - m)
    probs = (p / denom).astype(v.dtype)
    out = jnp.einsum('bnkgqs,bnskd->bnqkgd', probs, vb)
    return out.reshape(B, S, ATTN_WIDTH)


def conv_module(xc, conv_w, conv_b, ln_g, ln_b):
    a, gate = jnp.split(xc, 2, axis=-1)
    h = a * jax.nn.sigmoid(gate)
    h = lax.conv_general_dilated(
        h, conv_w[:, None, :].astype(h.dtype), window_strides=(1,),
        padding=[(CONV_KERNEL - 1, 0)],
        dimension_numbers=('NWC', 'WIO', 'NWC'),
        feature_group_count=CONV_WIDTH) + conv_b
    h = layer_norm(h, ln_g, ln_b)
    return jax.nn.silu(h)


def spatial_gating(xs, ln_g, ln_b, w_s, b_s):
    B, S = xs.shape[0], xs.shape[1]
    u, v = jnp.split(xs, 2, axis=-1)
    v = layer_norm(v, ln_g, ln_b)
    vb = v.reshape(B, S // CHUNK, CHUNK, SGU_HEADS, HEAD_DIM)
    causal = jnp.tril(jnp.ones((CHUNK, CHUNK), dtype=bool))
    w = jnp.where(causal[None], w_s, jnp.zeros_like(w_s))
    s = jnp.einsum('hij,bnjhd->bnihd', w, vb) + b_s.T[None, None, :, :, None]
    return u * s.reshape(B, S, SGU_WIDTH)


def setup_inputs(seed: int = 0) -> dict:
    key = jax.random.key(seed)
    ks = jax.random.split(key, 20)
    f32 = jnp.float32

    def nrm(k, shape, scale):
        return jax.random.normal(k, shape, f32) * scale

    def gain(k, shape):
        return 1.0 + 0.02 * jax.random.normal(k, shape, f32)

    return {
        "x": jax.random.normal(ks[0], (BATCH, SEQ, D_MODEL), f32),
        "ln1_g": gain(ks[1], (DEPTH, D_MODEL)),
        "w_in": nrm(ks[2], (DEPTH, D_MODEL, D_IN), D_MODEL ** -0.5),
        "q_norm_g": gain(ks[3], (DEPTH, HEAD_DIM)),
        "k_norm_g": gain(ks[4], (DEPTH, HEAD_DIM)),
        "sinks": nrm(ks[5], (DEPTH, N_Q_HEADS), 0.5),
        "conv_w": nrm(ks[6], (DEPTH, CONV_KERNEL, CONV_WIDTH), CONV_KERNEL ** -0.5),
        "conv_b": nrm(ks[7], (DEPTH, CONV_WIDTH), 0.02),
        "conv_ln_g": gain(ks[8], (DEPTH, CONV_WIDTH)),
        "conv_ln_b": nrm(ks[9], (DEPTH, CONV_WIDTH), 0.02),
        "sgu_ln_g": gain(ks[10], (DEPTH, SGU_WIDTH)),
        "sgu_ln_b": nrm(ks[11], (DEPTH, SGU_WIDTH), 0.02),
        "sgu_w": nrm(ks[12], (DEPTH, SGU_HEADS, CHUNK, CHUNK), CHUNK ** -0.5),
        "sgu_b": gain(ks[13], (DEPTH, SGU_HEADS, CHUNK)),
        "out_norm_g": gain(ks[14], (DEPTH, D_MIX)),
        "w_out": nrm(ks[15], (DEPTH, D_MIX, D_MODEL), D_MIX ** -0.5),
        "ln2_g": gain(ks[16], (DEPTH, D_MODEL)),
        "w_up": nrm(ks[17], (DEPTH, D_MODEL, D_FF), D_MODEL ** -0.5),
        "w_down": nrm(ks[18], (DEPTH, D_FF, D_MODEL), D_FF ** -0.5),
    }


def reference(x, ln1_g, w_in, q_norm_g, k_norm_g, sinks, conv_w, conv_b, conv_ln_g,
              conv_ln_b, sgu_ln_g, sgu_ln_b, sgu_w, sgu_b, out_norm_g, w_out, ln2_g,
              w_up, w_down):
    B, S = x.shape[0], x.shape[1]
    for l in range(DEPTH):
        h = rms_norm(x, ln1_g[l])
        proj = h @ w_in[l]
        q, k, v, xc, xs = jnp.split(proj, SPLITS, axis=-1)
        q = rms_norm(q.reshape(B, S, N_Q_HEADS, HEAD_DIM), q_norm_g[l])
        k = rms_norm(k.reshape(B, S, N_KV_HEADS, HEAD_DIM), k_norm_g[l])
        v = v.reshape(B, S, N_KV_HEADS, HEAD_DIM)
        y_attn = sliding_window_attention(q, k, v, sinks[l])
        y_conv = conv_module(xc, conv_w[l], conv_b[l], conv_ln_g[l], conv_ln_b[l])
        y_sgu = spatial_gating(xs, sgu_ln_g[l], sgu_ln_b[l], sgu_w[l], sgu_b[l])
        g = out_norm_g[l]
        mix = jnp.concatenate([
            rms_norm(y_attn, g[:ATTN_WIDTH]),
            rms_norm(y_conv, g[ATTN_WIDTH:ATTN_WIDTH + CONV_WIDTH]),
            rms_norm(y_sgu, g[ATTN_WIDTH + CONV_WIDTH:]),
        ], axis=-1)
        x = x + mix @ w_out[l]
        h = rms_norm(x, ln2_g[l])
        x = x + jnp.square(jax.nn.relu(h @ w_up[l])) @ w_down[l]
    return x
```

```python
import functools

import jax
import jax.numpy as jnp
import numpy as np
from jax import lax
from jax.experimental import pallas as pl
from jax.experimental.pallas import tpu as pltpu

F32 = jnp.float32
BF16 = jnp.bfloat16

D_MODEL = 2048
HEAD_DIM = 64
ATTN_WIDTH = 1024
N_Q_HEADS = 16
N_KV_HEADS = 4
GROUP = N_Q_HEADS // N_KV_HEADS
KV_WIDTH = 256
WINDOW = 128
CONV_WIDTH = 512
CONV_KERNEL = 31
CONV_HALO = 32
SGU_WIDTH = 512
SGU_HEADS = 8
CHUNK = 128
D_FF = 4 * D_MODEL
D_IN = ATTN_WIDTH + 2 * KV_WIDTH + 2 * CONV_WIDTH + 2 * SGU_WIDTH
EPS = 1e-6
NEG_INF = -1e30
LANES = 128
HALF = LANES // 2

VMEM_LIMIT = 56 * 1024 * 1024


def _rms_scale(x):
    return lax.rsqrt(jnp.mean(x * x, axis=-1, keepdims=True) + EPS)


def _layer_norm(x, g, b):
    mu = jnp.mean(x, axis=-1, keepdims=True)
    xc = x - mu
    return xc * lax.rsqrt(jnp.mean(xc * xc, axis=-1, keepdims=True) + EPS) * g + b


def _in_proj_kernel(x_ref, g_ref, w_ref, o_ref):
    x = x_ref[...]
    h = (x * _rms_scale(x) * g_ref[...]).astype(BF16)
    o_ref[...] = jnp.dot(h, w_ref[...], preferred_element_type=F32)


def _in_proj(x2, g, w, tm):
    T = x2.shape[0]
    return pl.pallas_call(
        _in_proj_kernel,
        grid=(T // tm,),
        in_specs=[
            pl.BlockSpec((tm, D_MODEL), lambda i: (i, 0)),
            pl.BlockSpec((1, D_MODEL), lambda i: (0, 0)),
            pl.BlockSpec((D_MODEL, D_IN), lambda i: (0, 0), pipeline_mode=pl.Buffered(1)),
        ],
        out_specs=pl.BlockSpec((tm, D_IN), lambda i: (i, 0)),
        out_shape=jax.ShapeDtypeStruct((T, D_IN), F32),
        compiler_params=pltpu.CompilerParams(
            dimension_semantics=("arbitrary",), vmem_limit_bytes=VMEM_LIMIT),
        name="in_proj",
    )(x2, g, w)


def _head_sumsq(x, ones_bd):
    x2 = x * x
    hi = x2.astype(BF16)
    lo = (x2 - hi.astype(F32)).astype(BF16)
    return (jnp.dot(hi, ones_bd, preferred_element_type=F32)
            + jnp.dot(lo, ones_bd, preferred_element_type=F32))


def _mixers_kernel(q_ref, k_ref, v_ref, ca_ref, cg_ref, su_ref, sv_ref,
                   gq_ref, gk_ref, sinks_ref, cw_ref, cb_ref, clg_ref, clb_ref,
                   slg_ref, slb_ref, sw_ref, sb_ref, og_ref,
                   o_ref,
                   krep_ref, vwin_ref, hbuf_ref, *, tm):
    t = pl.program_id(1)

    @pl.when(t == 0)
    def _():
        krep_ref[...] = jnp.zeros_like(krep_ref)
        vwin_ref[...] = jnp.zeros_like(vwin_ref)
        hbuf_ref[0:CONV_HALO, :] = jnp.zeros((CONV_HALO, CONV_WIDTH), F32)

    lane = lax.broadcasted_iota(jnp.int32, (WINDOW, LANES), 1)
    low_half = lane < HALF
    r256 = lax.broadcasted_iota(jnp.int32, (KV_WIDTH, KV_WIDTH), 0)
    c256 = lax.broadcasted_iota(jnp.int32, (KV_WIDTH, KV_WIDTH), 1)
    ones_bd = jnp.where((r256 // HEAD_DIM) == (c256 // HEAD_DIM), 1.0, 0.0).astype(BF16)
    qi = lax.broadcasted_iota(jnp.int32, (WINDOW, 2 * WINDOW), 0)
    sj_raw = lax.broadcasted_iota(jnp.int32, (WINDOW, 2 * WINDOW), 1)
    colblk = lax.broadcasted_iota(jnp.int32, (WINDOW, KV_WIDTH), 1) // HEAD_DIM
    og = og_ref[...]

    gq = gq_ref[...] * (HEAD_DIM ** -0.5)
    gk = gk_ref[...]
    for s in range(tm // WINDOW):
        rows = pl.ds(s * WINDOW, WINDOW)
        k = k_ref[rows, :]
        kn = k * lax.rsqrt(_head_sumsq(k, ones_bd) * (1.0 / HEAD_DIM) + EPS) * gk
        cur = pl.ds((s % 2) * WINDOW, WINDOW)
        vwin_ref[cur, :] = v_ref[rows, :].astype(BF16)
        for p in range(N_KV_HEADS // 2):
            blk = kn[:, p * LANES:(p + 1) * LANES]
            swapped = pltpu.roll(blk, HALF, 1)
            even = jnp.where(low_half, blk, swapped).astype(BF16)
            odd = jnp.where(low_half, swapped, blk).astype(BF16)
            krep_ref[2 * p, cur, :] = jnp.concatenate([even, even], axis=1)
            krep_ref[2 * p + 1, cur, :] = jnp.concatenate([odd, odd], axis=1)

        sj = (sj_raw + WINDOW) % (2 * WINDOW) if s % 2 == 0 else sj_raw
        rel = qi + WINDOW - sj
        mask = (rel >= 0) & (rel < WINDOW)
        if s == 0:
            mask = mask & (sj >= (1 - jnp.minimum(t, 1)) * WINDOW)
        mask4 = jnp.concatenate([mask] * GROUP, axis=0)
        vwin = vwin_ref[...]
        outs = []
        for kh in range(N_KV_HEADS):
            qs = q_ref[rows, kh * KV_WIDTH:(kh + 1) * KV_WIDTH]
            qn = (qs * lax.rsqrt(_head_sumsq(qs, ones_bd) * (1.0 / HEAD_DIM) + EPS)
                  * gq[:, kh * KV_WIDTH:(kh + 1) * KV_WIDTH]).astype(BF16)
            a = jnp.concatenate(
                [jnp.where(colblk == g, qn, jnp.zeros_like(qn)) for g in range(GROUP)], axis=0)
            logits = lax.dot_general(a, krep_ref[kh], (((1,), (1,)), ((), ())),
                                     preferred_element_type=F32)
            logits = jnp.where(mask4, logits, NEG_INF)
            sink = jnp.concatenate(
                [jnp.full((WINDOW, 1), sinks_ref[kh * GROUP + g], F32) for g in range(GROUP)],
                axis=0)
            m = jnp.maximum(jnp.max(logits, axis=-1, keepdims=True), sink)
            pexp = jnp.exp(logits - m)
            denom = jnp.sum(pexp, axis=-1, keepdims=True) + jnp.exp(sink - m)
            probs = (pexp * (1.0 / denom)).astype(BF16)
            outs.append(jnp.dot(probs, vwin, preferred_element_type=F32))
        y = jnp.concatenate(
            [jnp.where(low_half,
                       outs[2 * p][g * WINDOW:(g + 1) * WINDOW, p * LANES:(p + 1) * LANES],
                       outs[2 * p + 1][g * WINDOW:(g + 1) * WINDOW, p * LANES:(p + 1) * LANES])
             for p in range(N_KV_HEADS // 2) for g in range(GROUP)], axis=1)
        o_ref[rows, 0:ATTN_WIDTH] = (y * _rms_scale(y) * og[:, 0:ATTN_WIDTH]).astype(BF16)

    hbuf_ref[CONV_HALO:CONV_HALO + tm, :] = ca_ref[...] * jax.nn.sigmoid(cg_ref[...])
    conv_rows = 32
    for r0 in range(0, tm, conv_rows):
        acc = jnp.zeros((conv_rows, CONV_WIDTH), F32)
        for j in range(CONV_KERNEL):
            off = CONV_HALO - (CONV_KERNEL - 1) + j + r0
            acc = acc + hbuf_ref[off:off + conv_rows, :] * cw_ref[j:j + 1, :]
        hc = _layer_norm(acc + cb_ref[...], clg_ref[...], clb_ref[...])
        yc = hc * jax.nn.sigmoid(hc)
        o_ref[r0:r0 + conv_rows, ATTN_WIDTH:ATTN_WIDTH + CONV_WIDTH] = (
            yc * _rms_scale(yc) * og[:, ATTN_WIDTH:ATTN_WIDTH + CONV_WIDTH]).astype(BF16)
    hbuf_ref[0:CONV_HALO, :] = hbuf_ref[tm:tm + CONV_HALO, :]

    ri = lax.broadcasted_iota(jnp.int32, (2 * CHUNK, CHUNK), 0) % CHUNK
    cj = lax.broadcasted_iota(jnp.int32, (2 * CHUNK, CHUNK), 1)
    causal = cj <= ri
    wpairs = [jnp.where(causal, sw_ref[p], 0.0).astype(BF16) for p in range(SGU_HEADS // 2)]
    for c in range(tm // CHUNK):
        rows = pl.ds(c * CHUNK, CHUNK)
        vn = _layer_norm(sv_ref[rows, :], slg_ref[...], slb_ref[...]).astype(BF16)
        ys = []
        for p in range(SGU_HEADS // 2):
            r = jnp.dot(wpairs[p], vn[:, p * LANES:(p + 1) * LANES],
                        preferred_element_type=F32)
            sp = jnp.where(low_half, r[0:CHUNK], r[CHUNK:2 * CHUNK])
            sp = sp + sb_ref[:, p * LANES:(p + 1) * LANES]
            ys.append(su_ref[rows, p * LANES:(p + 1) * LANES] * sp)
        ysg = jnp.concatenate(ys, axis=1)
        o_ref[rows, ATTN_WIDTH + CONV_WIDTH:] = (
            ysg * _rms_scale(ysg) * og[:, ATTN_WIDTH + CONV_WIDTH:]).astype(BF16)


def _mixers(proj, B, S, tm, gq, gk, sinks, cw, cb, clg, clb, slg, slb, sw, sb, og):
    T = B * S
    nt = S // tm
    assert (tm // WINDOW) % 2 == 0, "window halves alternate per block and must realign every tile"

    def col(width, blk):
        return pl.BlockSpec((tm, width), lambda b, t: (b * nt + t, blk))

    def whole(shape):
        return pl.BlockSpec(shape, lambda b, t: (0,) * len(shape))

    return pl.pallas_call(
        functools.partial(_mixers_kernel, tm=tm),
        grid=(B, nt),
        in_specs=[
            col(ATTN_WIDTH, 0),
            col(KV_WIDTH, 4),
            col(KV_WIDTH, 5),
            col(CONV_WIDTH, 3),
            col(CONV_WIDTH, 4),
            col(SGU_WIDTH, 5),
            col(SGU_WIDTH, 6),
            whole((1, ATTN_WIDTH)), whole((1, KV_WIDTH)),
            pl.BlockSpec(memory_space=pltpu.MemorySpace.SMEM),
            whole((CONV_KERNEL, CONV_WIDTH)), whole((1, CONV_WIDTH)),
            whole((1, CONV_WIDTH)), whole((1, CONV_WIDTH)),
            whole((1, SGU_WIDTH)), whole((1, SGU_WIDTH)),
            whole((SGU_HEADS // 2, 2 * CHUNK, CHUNK)), whole((CHUNK, SGU_WIDTH)),
            whole((1, D_MODEL)),
        ],
        out_specs=pl.BlockSpec((tm, D_MODEL), lambda b, t: (b * nt + t, 0)),
        out_shape=jax.ShapeDtypeStruct((T, D_MODEL), BF16),
        scratch_shapes=[
            pltpu.VMEM((N_KV_HEADS, 2 * WINDOW, KV_WIDTH), BF16),
            pltpu.VMEM((2 * WINDOW, KV_WIDTH), BF16),
            pltpu.VMEM((CONV_HALO + tm, CONV_WIDTH), F32),
        ],
        compiler_params=pltpu.CompilerParams(
            dimension_semantics=("arbitrary", "arbitrary"), vmem_limit_bytes=VMEM_LIMIT),
        name="mixers",
    )(proj, proj, proj, proj, proj, proj, proj,
      gq, gk, sinks, cw, cb, clg, clb, slg, slb, sw, sb, og)


def _out_proj_kernel(x_ref, m_ref, w_ref, o_ref):
    o_ref[...] = x_ref[...] + jnp.dot(m_ref[...], w_ref[...], preferred_element_type=F32)


def _out_proj(x2, mix, w, tm):
    T = x2.shape[0]
    return pl.pallas_call(
        _out_proj_kernel,
        grid=(T // tm,),
        in_specs=[
            pl.BlockSpec((tm, D_MODEL), lambda i: (i, 0)),
            pl.BlockSpec((tm, D_MODEL), lambda i: (i, 0)),
            pl.BlockSpec((D_MODEL, D_MODEL), lambda i: (0, 0), pipeline_mode=pl.Buffered(1)),
        ],
        out_specs=pl.BlockSpec((tm, D_MODEL), lambda i: (i, 0)),
        out_shape=jax.ShapeDtypeStruct((T, D_MODEL), F32),
        compiler_params=pltpu.CompilerParams(
            dimension_semantics=("arbitrary",), vmem_limit_bytes=VMEM_LIMIT),
        name="out_proj",
    )(x2, mix, w)


def _mlp_kernel(x_ref, g_ref, wu_ref, wd_ref, o_ref, h_ref):
    f = pl.program_id(1)

    @pl.when(f == 0)
    def _():
        x = x_ref[...]
        h_ref[...] = (x * _rms_scale(x) * g_ref[...]).astype(BF16)
        o_ref[...] = x

    u = jnp.dot(h_ref[...], wu_ref[...], preferred_element_type=F32)
    u = jnp.maximum(u, 0.0)
    o_ref[...] += jnp.dot((u * u).astype(BF16), wd_ref[...], preferred_element_type=F32)


def _mlp(x2, g, wu, wd, tm, tf):
    T = x2.shape[0]
    return pl.pallas_call(
        _mlp_kernel,
        grid=(T // tm, D_FF // tf),
        in_specs=[
            pl.BlockSpec((tm, D_MODEL), lambda i, f: (i, 0)),
            pl.BlockSpec((1, D_MODEL), lambda i, f: (0, 0)),
            pl.BlockSpec((D_MODEL, tf), lambda i, f: (0, f)),
            pl.BlockSpec((tf, D_MODEL), lambda i, f: (f, 0)),
        ],
        out_specs=pl.BlockSpec((tm, D_MODEL), lambda i, f: (i, 0)),
        out_shape=jax.ShapeDtypeStruct((T, D_MODEL), F32),
        scratch_shapes=[pltpu.VMEM((tm, D_MODEL), BF16)],
        compiler_params=pltpu.CompilerParams(
            dimension_semantics=("arbitrary", "arbitrary"), vmem_limit_bytes=VMEM_LIMIT),
        name="mlp",
    )(x2, g, wu, wd)


def _attn_perm():
    perm = np.empty((ATTN_WIDTH,), np.int32)
    for p in range(N_KV_HEADS // 2):
        for g in range(GROUP):
            for e in range(2):
                head = GROUP * (2 * p + e) + g
                dst = (GROUP * p + g) * LANES + e * HALF
                perm[dst:dst + HEAD_DIM] = np.arange(head * HEAD_DIM, (head + 1) * HEAD_DIM)
    return perm


def _pick_tile(n, pref):
    while n % pref:
        pref //= 2
    return pref


def kernel(x, ln1_g, w_in, q_norm_g, k_norm_g, sinks, conv_w, conv_b, conv_ln_g, conv_ln_b,
           sgu_ln_g, sgu_ln_b, sgu_w, sgu_b, out_norm_g, w_out, ln2_g, w_up, w_down):
    B, S, D = x.shape
    depth = w_in.shape[0]
    assert D == D_MODEL and S % WINDOW == 0
    T = B * S
    tm_proj = _pick_tile(T, 512)
    tm_mix = _pick_tile(S, 256)
    tm_mlp = _pick_tile(T, 512)
    perm = np.concatenate([_attn_perm(), np.arange(ATTN_WIDTH, D_MODEL, dtype=np.int32)])

    x2 = x.reshape(T, D)
    for l in range(depth):
        proj = _in_proj(x2, ln1_g[l][None, :], w_in[l].astype(BF16), tm_proj)
        mix = _mixers(
            proj, B, S, tm_mix,
            jnp.tile(q_norm_g[l], N_Q_HEADS)[None, :], jnp.tile(k_norm_g[l], N_KV_HEADS)[None, :],
            sinks[l], conv_w[l], conv_b[l][None, :], conv_ln_g[l][None, :], conv_ln_b[l][None, :],
            sgu_ln_g[l][None, :], sgu_ln_b[l][None, :],
            sgu_w[l].reshape(SGU_HEADS // 2, 2 * CHUNK, CHUNK),
            jnp.repeat(sgu_b[l].T, HEAD_DIM, axis=1),
            out_norm_g[l][perm][None, :])
        x2 = _out_proj(x2, mix, w_out[l][perm, :].astype(BF16), tm_proj)
        x2 = _mlp(x2, ln2_g[l][None, :], w_up[l].astype(BF16), w_down[l].astype(BF16),
                  tm_mlp, 1024)
    return x2.reshape(B, S, D)
```

```python
import functools

import jax
import jax.numpy as jnp
import numpy as np
from jax import lax
from jax.experimental import pallas as pl
from jax.experimental.pallas import tpu as pltpu

F32 = jnp.float32
BF16 = jnp.bfloat16

D_MODEL = 2048
HEAD_DIM = 64
ATTN_WIDTH = 1024
N_Q_HEADS = 16
N_KV_HEADS = 4
GROUP = N_Q_HEADS // N_KV_HEADS
KV_WIDTH = 256
WINDOW = 128
CONV_WIDTH = 512
CONV_KERNEL = 31
CONV_HALO = 32
SGU_WIDTH = 512
SGU_HEADS = 8
CHUNK = 128
D_FF = 4 * D_MODEL
D_IN = ATTN_WIDTH + 2 * KV_WIDTH + 2 * CONV_WIDTH + 2 * SGU_WIDTH
EPS = 1e-6
NEG_INF = -1e30
LANES = 128
HALF = LANES // 2
SUBLANES = 8
SOFTMAX_ROWS = 64
CONV_ROWS = 64

VMEM_LIMIT = 56 * 1024 * 1024


def _rms_scale(x):
    return lax.rsqrt(jnp.mean(x * x, axis=-1, keepdims=True) + EPS)


def _layer_norm(x, g, b):
    mu = jnp.mean(x, axis=-1, keepdims=True)
    xc = x - mu
    return xc * lax.rsqrt(jnp.mean(xc * xc, axis=-1, keepdims=True) + EPS) * g + b


def _in_proj_kernel(x_ref, g_ref, w_ref, o_ref):
    x = x_ref[...]
    h = (x * _rms_scale(x) * g_ref[...]).astype(BF16)
    o_ref[...] = jnp.dot(h, w_ref[...], preferred_element_type=F32)


def _in_proj(x2, g, w, tm):
    T = x2.shape[0]
    return pl.pallas_call(
        _in_proj_kernel,
        grid=(T // tm,),
        in_specs=[
            pl.BlockSpec((tm, D_MODEL), lambda i: (i, 0)),
            pl.BlockSpec((1, D_MODEL), lambda i: (0, 0)),
            pl.BlockSpec((D_MODEL, D_IN), lambda i: (0, 0), pipeline_mode=pl.Buffered(1)),
        ],
        out_specs=pl.BlockSpec((tm, D_IN), lambda i: (i, 0)),
        out_shape=jax.ShapeDtypeStruct((T, D_IN), F32),
        compiler_params=pltpu.CompilerParams(
            dimension_semantics=("arbitrary",), vmem_limit_bytes=VMEM_LIMIT),
        name="in_proj",
    )(x2, g, w)


def _head_sumsq(x, ones_bd):
    x2 = x * x
    hi = x2.astype(BF16)
    lo = (x2 - hi.astype(F32)).astype(BF16)
    return (jnp.dot(hi, ones_bd, preferred_element_type=F32)
            + jnp.dot(lo, ones_bd, preferred_element_type=F32))


def _mixers_kernel(q_ref, k_ref, v_ref, ca_ref, cg_ref, su_ref, sv_ref,
                   gq_ref, gk_ref, sinks_ref, cw_ref, cb_ref, clg_ref, clb_ref,
                   slg_ref, slb_ref, sw_ref, sb_ref, og_ref,
                   o_ref,
                   krep_ref, vwin_ref, hbuf_ref, tail_ref, hs_ref, w8_ref, wsg_ref, ones_ref, mask_ref,
                   *, tm):
    b = pl.program_id(0)
    t = pl.program_id(1)
    lane = lax.broadcasted_iota(jnp.int32, (WINDOW, LANES), 1)
    low_half = lane < HALF
    qi = lax.broadcasted_iota(jnp.int32, (WINDOW, 2 * WINDOW), 0)
    sj_raw = lax.broadcasted_iota(jnp.int32, (WINDOW, 2 * WINDOW), 1)

    def band(sj):
        rel = qi + WINDOW - sj
        return (rel >= 0) & (rel < WINDOW)

    @pl.when((b == 0) & (t == 0))
    def _():
        r256 = lax.broadcasted_iota(jnp.int32, (KV_WIDTH, KV_WIDTH), 0)
        c256 = lax.broadcasted_iota(jnp.int32, (KV_WIDTH, KV_WIDTH), 1)
        ones_ref[...] = jnp.where((r256 // HEAD_DIM) == (c256 // HEAD_DIM), 1.0, 0.0).astype(BF16)
        for j in range(CONV_KERNEL):
            w8_ref[j] = jnp.broadcast_to(cw_ref[j:j + 1, :], (SUBLANES, CONV_WIDTH))
        ri = lax.broadcasted_iota(jnp.int32, (2 * CHUNK, CHUNK), 0) % CHUNK
        cj = lax.broadcasted_iota(jnp.int32, (2 * CHUNK, CHUNK), 1)
        for p in range(SGU_HEADS // 2):
            wsg_ref[p] = jnp.where(cj <= ri, sw_ref[p], 0.0).astype(BF16)
        mask_ref[1] = jnp.where(band(sj_raw), 1.0, 0.0)

    @pl.when(t == 0)
    def _():
        krep_ref[...] = jnp.zeros_like(krep_ref)
        vwin_ref[...] = jnp.zeros_like(vwin_ref)
        tail_ref[1] = jnp.zeros((CONV_HALO, CONV_WIDTH), F32)

    sj_even = (sj_raw + WINDOW) % (2 * WINDOW)
    mask_ref[0] = jnp.where(band(sj_even) & (sj_even >= (1 - jnp.minimum(t, 1)) * WINDOW), 1.0, 0.0)

    ones_bd = ones_ref[...]
    colblk = lax.broadcasted_iota(jnp.int32, (WINDOW, KV_WIDTH), 1) // HEAD_DIM
    og = og_ref[...]

    gq = gq_ref[...] * (HEAD_DIM ** -0.5)
    gk = gk_ref[...]
    parity = t % 2
    for s in range(tm // WINDOW):
        rows = pl.ds(s * WINDOW, WINDOW)
        k = k_ref[rows, :]
        kn = k * lax.rsqrt(_head_sumsq(k, ones_bd) * (1.0 / HEAD_DIM) + EPS) * gk
        cur = pl.ds(pl.multiple_of(parity * WINDOW, WINDOW), WINDOW)
        vwin_ref[cur, :] = v_ref[rows, :].astype(BF16)
        for p in range(N_KV_HEADS // 2):
            blk = kn[:, p * LANES:(p + 1) * LANES]
            swapped = pltpu.roll(blk, HALF, 1)
            even = jnp.where(low_half, blk, swapped).astype(BF16)
            odd = jnp.where(low_half, swapped, blk).astype(BF16)
            krep_ref[2 * p, cur, :] = jnp.concatenate([even, even], axis=1)
            krep_ref[2 * p + 1, cur, :] = jnp.concatenate([odd, odd], axis=1)

        vwin = vwin_ref[...]
        outs = []
        for kh in range(N_KV_HEADS):
            qs = q_ref[rows, kh * KV_WIDTH:(kh + 1) * KV_WIDTH]
            qn = (qs * lax.rsqrt(_head_sumsq(qs, ones_bd) * (1.0 / HEAD_DIM) + EPS)
                  * gq[:, kh * KV_WIDTH:(kh + 1) * KV_WIDTH]).astype(BF16)
            a = jnp.concatenate(
                [jnp.where(colblk == g, qn, jnp.zeros_like(qn)) for g in range(GROUP)], axis=0)
            logits = lax.dot_general(a, krep_ref[kh], (((1,), (1,)), ((), ())),
                                     preferred_element_type=F32)
            probs = []
            for r0 in range(0, GROUP * WINDOW, SOFTMAX_ROWS):
                q0 = r0 % WINDOW
                valid = mask_ref[parity, q0:q0 + SOFTMAX_ROWS, :] > 0.5
                lg = jnp.where(valid, logits[r0:r0 + SOFTMAX_ROWS], NEG_INF)
                sink = sinks_ref[kh * GROUP + r0 // WINDOW]
                m = jnp.maximum(jnp.max(lg, axis=-1, keepdims=True), sink)
                pexp = jnp.exp(lg - m)
                denom = jnp.sum(pexp, axis=-1, keepdims=True) + jnp.exp(sink - m)
                probs.append((pexp * (1.0 / denom)).astype(BF16))
            outs.append(jnp.dot(jnp.concatenate(probs, axis=0), vwin,
                                preferred_element_type=F32))
        y = jnp.concatenate(
            [jnp.where(low_half,
                       outs[2 * p][g * WINDOW:(g + 1) * WINDOW, p * LANES:(p + 1) * LANES],
                       outs[2 * p + 1][g * WINDOW:(g + 1) * WINDOW, p * LANES:(p + 1) * LANES])
             for p in range(N_KV_HEADS // 2) for g in range(GROUP)], axis=1)
        o_ref[rows, 0:ATTN_WIDTH] = (y * _rms_scale(y) * og[:, 0:ATTN_WIDTH]).astype(BF16)

    hbuf_ref[0:CONV_HALO, :] = tail_ref[1 - parity]
    hbuf_ref[CONV_HALO:CONV_HALO + tm, :] = ca_ref[...] * jax.nn.sigmoid(cg_ref[...])
    for r in range(1, SUBLANES):
        hs_ref[r - 1] = hbuf_ref[r:r + tm + CONV_HALO - SUBLANES, :]
    groups = CONV_ROWS // SUBLANES
    for r0 in range(0, tm, CONV_ROWS):
        accs = [None] * groups
        for j in range(CONV_KERNEL):
            a, r = divmod(CONV_HALO - (CONV_KERNEL - 1) + j, SUBLANES)
            w8 = w8_ref[j]
            for g in range(groups):
                base = r0 + SUBLANES * (a + g)
                if r == 0:
                    src = hbuf_ref[base:base + SUBLANES, :]
                else:
                    src = hs_ref[r - 1, base:base + SUBLANES, :]
                accs[g] = src * w8 if accs[g] is None else accs[g] + src * w8
        acc = jnp.concatenate(accs, axis=0)
        hc = _layer_norm(acc + cb_ref[...], clg_ref[...], clb_ref[...])
        yc = hc * jax.nn.sigmoid(hc)
        o_ref[r0:r0 + CONV_ROWS, ATTN_WIDTH:ATTN_WIDTH + CONV_WIDTH] = (
            yc * _rms_scale(yc) * og[:, ATTN_WIDTH:ATTN_WIDTH + CONV_WIDTH]).astype(BF16)
    tail_ref[parity] = hbuf_ref[tm:tm + CONV_HALO, :]

    for c in range(tm // CHUNK):
        rows = pl.ds(c * CHUNK, CHUNK)
        vn = _layer_norm(sv_ref[rows, :], slg_ref[...], slb_ref[...]).astype(BF16)
        ys = []
        for p in range(SGU_HEADS // 2):
            r = jnp.dot(wsg_ref[p], vn[:, p * LANES:(p + 1) * LANES],
                        preferred_element_type=F32)
            sp = jnp.where(low_half, r[0:CHUNK], r[CHUNK:2 * CHUNK])
            sp = sp + sb_ref[:, p * LANES:(p + 1) * LANES]
            ys.append(su_ref[rows, p * LANES:(p + 1) * LANES] * sp)
        ysg = jnp.concatenate(ys, axis=1)
        o_ref[rows, ATTN_WIDTH + CONV_WIDTH:] = (
            ysg * _rms_scale(ysg) * og[:, ATTN_WIDTH + CONV_WIDTH:]).astype(BF16)


def _mixers(proj, B, S, tm, gq, gk, sinks, cw, cb, clg, clb, slg, slb, sw, sb, og):
    T = B * S
    nt = S // tm
    assert tm == WINDOW, "one attention block per grid step"

    def col(width, blk):
        return pl.BlockSpec((tm, width), lambda b, t: (b * nt + t, blk))

    def whole(shape):
        return pl.BlockSpec(shape, lambda b, t: (0,) * len(shape))

    return pl.pallas_call(
        functools.partial(_mixers_kernel, tm=tm),
        grid=(B, nt),
        in_specs=[
            col(ATTN_WIDTH, 0),
            col(KV_WIDTH, 4),
            col(KV_WIDTH, 5),
            col(CONV_WIDTH, 3),
            col(CONV_WIDTH, 4),
            col(SGU_WIDTH, 5),
            col(SGU_WIDTH, 6),
            whole((1, ATTN_WIDTH)), whole((1, KV_WIDTH)),
            pl.BlockSpec(memory_space=pltpu.MemorySpace.SMEM),
            whole((CONV_KERNEL, CONV_WIDTH)), whole((1, CONV_WIDTH)),
            whole((1, CONV_WIDTH)), whole((1, CONV_WIDTH)),
            whole((1, SGU_WIDTH)), whole((1, SGU_WIDTH)),
            whole((SGU_HEADS // 2, 2 * CHUNK, CHUNK)), whole((CHUNK, SGU_WIDTH)),
            whole((1, D_MODEL)),
        ],
        out_specs=pl.BlockSpec((tm, D_MODEL), lambda b, t: (b * nt + t, 0)),
        out_shape=jax.ShapeDtypeStruct((T, D_MODEL), BF16),
        scratch_shapes=[
            pltpu.VMEM((N_KV_HEADS, 2 * WINDOW, KV_WIDTH), BF16),
            pltpu.VMEM((2 * WINDOW, KV_WIDTH), BF16),
            pltpu.VMEM((CONV_HALO + tm, CONV_WIDTH), F32),
            pltpu.VMEM((2, CONV_HALO, CONV_WIDTH), F32),
            pltpu.VMEM((SUBLANES - 1, CONV_HALO + tm - SUBLANES, CONV_WIDTH), F32),
            pltpu.VMEM((CONV_KERNEL, SUBLANES, CONV_WIDTH), F32),
            pltpu.VMEM((SGU_HEADS // 2, 2 * CHUNK, CHUNK), BF16),
            pltpu.VMEM((KV_WIDTH, KV_WIDTH), BF16),
            pltpu.VMEM((2, WINDOW, 2 * WINDOW), F32),
        ],
        compiler_params=pltpu.CompilerParams(
            dimension_semantics=("arbitrary", "arbitrary"), vmem_limit_bytes=VMEM_LIMIT),
        name="mixers",
    )(proj, proj, proj, proj, proj, proj, proj,
      gq, gk, sinks, cw, cb, clg, clb, slg, slb, sw, sb, og)


def _out_proj_kernel(x_ref, m_ref, w_ref, o_ref):
    o_ref[...] = x_ref[...] + jnp.dot(m_ref[...], w_ref[...], preferred_element_type=F32)


def _out_proj(x2, mix, w, tm):
    T = x2.shape[0]
    return pl.pallas_call(
        _out_proj_kernel,
        grid=(T // tm,),
        in_specs=[
            pl.BlockSpec((tm, D_MODEL), lambda i: (i, 0)),
            pl.BlockSpec((tm, D_MODEL), lambda i: (i, 0)),
            pl.BlockSpec((D_MODEL, D_MODEL), lambda i: (0, 0), pipeline_mode=pl.Buffered(1)),
        ],
        out_specs=pl.BlockSpec((tm, D_MODEL), lambda i: (i, 0)),
        out_shape=jax.ShapeDtypeStruct((T, D_MODEL), F32),
        compiler_params=pltpu.CompilerParams(
            dimension_semantics=("arbitrary",), vmem_limit_bytes=VMEM_LIMIT),
        name="out_proj",
    )(x2, mix, w)


def _mlp_kernel(x_ref, g_ref, wu_ref, wd_ref, o_ref, h_ref):
    f = pl.program_id(1)

    @pl.when(f == 0)
    def _():
        x = x_ref[...]
        h_ref[...] = (x * _rms_scale(x) * g_ref[...]).astype(BF16)
        o_ref[...] = x

    u = jnp.dot(h_ref[...], wu_ref[...], preferred_element_type=F32)
    u = jnp.maximum(u, 0.0)
    o_ref[...] += jnp.dot((u * u).astype(BF16), wd_ref[...], preferred_element_type=F32)


def _mlp(x2, g, wu, wd, tm, tf):
    T = x2.shape[0]
    return pl.pallas_call(
        _mlp_kernel,
        grid=(T // tm, D_FF // tf),
        in_specs=[
            pl.BlockSpec((tm, D_MODEL), lambda i, f: (i, 0)),
            pl.BlockSpec((1, D_MODEL), lambda i, f: (0, 0)),
            pl.BlockSpec((D_MODEL, tf), lambda i, f: (0, f)),
            pl.BlockSpec((tf, D_MODEL), lambda i, f: (f, 0)),
        ],
        out_specs=pl.BlockSpec((tm, D_MODEL), lambda i, f: (i, 0)),
        out_shape=jax.ShapeDtypeStruct((T, D_MODEL), F32),
        scratch_shapes=[pltpu.VMEM((tm, D_MODEL), BF16)],
        compiler_params=pltpu.CompilerParams(
            dimension_semantics=("arbitrary", "arbitrary"), vmem_limit_bytes=VMEM_LIMIT),
        name="mlp",
    )(x2, g, wu, wd)


def _attn_perm():
    perm = np.empty((ATTN_WIDTH,), np.int32)
    for p in range(N_KV_HEADS // 2):
        for g in range(GROUP):
            for e in range(2):
                head = GROUP * (2 * p + e) + g
                dst = (GROUP * p + g) * LANES + e * HALF
                perm[dst:dst + HEAD_DIM] = np.arange(head * HEAD_DIM, (head + 1) * HEAD_DIM)
    return perm


def _pick_tile(n, pref):
    while n % pref:
        pref //= 2
    return pref


def kernel(x, ln1_g, w_in, q_norm_g, k_norm_g, sinks, conv_w, conv_b, conv_ln_g, conv_ln_b,
           sgu_ln_g, sgu_ln_b, sgu_w, sgu_b, out_norm_g, w_out, ln2_g, w_up, w_down):
    B, S, D = x.shape
    depth = w_in.shape[0]
    assert D == D_MODEL and S % WINDOW == 0
    T = B * S
    tm_proj = _pick_tile(T, 512)
    tm_mix = WINDOW
    tm_mlp = _pick_tile(T, 512)
    perm = np.concatenate([_attn_perm(), np.arange(ATTN_WIDTH, D_MODEL, dtype=np.int32)])

    x2 = x.reshape(T, D)
    for l in range(depth):
        proj = _in_proj(x2, ln1_g[l][None, :], w_in[l].astype(BF16), tm_proj)
        mix = _mixers(
            proj, B, S, tm_mix,
            jnp.tile(q_norm_g[l], N_Q_HEADS)[None, :], jnp.tile(k_norm_g[l], N_KV_HEADS)[None, :],
            sinks[l], conv_w[l], conv_b[l][None, :], conv_ln_g[l][None, :], conv_ln_b[l][None, :],
            sgu_ln_g[l][None, :], sgu_ln_b[l][None, :],
            sgu_w[l].reshape(SGU_HEADS // 2, 2 * CHUNK, CHUNK),
            jnp.repeat(sgu_b[l].T, HEAD_DIM, axis=1),
            out_norm_g[l][perm][None, :])
        x2 = _out_proj(x2, mix, w_out[l][perm, :].astype(BF16), tm_proj)
        x2 = _mlp(x2, ln2_g[l][None, :], w_up[l].astype(BF16), w_down[l].astype(BF16),
                  tm_mlp, 1024)
    return x2.reshape(B, S, D)
```

```python
import functools

import jax
import jax.numpy as jnp
import numpy as np
from jax import lax
from jax.experimental import pallas as pl
from jax.experimental.pallas import tpu as pltpu

F32 = jnp.float32
BF16 = jnp.bfloat16

D_MODEL = 2048
HEAD_DIM = 64
ATTN_WIDTH = 1024
N_Q_HEADS = 16
N_KV_HEADS = 4
GROUP = N_Q_HEADS // N_KV_HEADS
KV_WIDTH = 256
WINDOW = 128
CONV_WIDTH = 512
CONV_KERNEL = 31
CONV_HALO = 32
SGU_WIDTH = 512
SGU_HEADS = 8
CHUNK = 128
D_FF = 4 * D_MODEL
D_IN = ATTN_WIDTH + 2 * KV_WIDTH + 2 * CONV_WIDTH + 2 * SGU_WIDTH
EPS = 1e-6
NEG_INF = -1e30
LANES = 128
HALF = LANES // 2
SUBLANES = 8
SOFTMAX_ROWS = 64
CONV_ROWS = 32
BLOCK = WINDOW
MXU_TILE = 256
_DONE = object()
CONV_TAPS_PER_PIECE = 8
ATTENTION_PIECES = 2 + 2 * N_KV_HEADS
CONV_PIECES = 1 + (BLOCK // CONV_ROWS) * -(-CONV_KERNEL // CONV_TAPS_PER_PIECE)
MLP_PIECES = 16

VMEM_LIMIT = 60 * 1024 * 1024


def _rms_scale(x):
    return lax.rsqrt(jnp.mean(x * x, axis=-1, keepdims=True) + EPS)


def _layer_norm(x, g, b):
    mu = jnp.mean(x, axis=-1, keepdims=True)
    xc = x - mu
    return xc * lax.rsqrt(jnp.mean(xc * xc, axis=-1, keepdims=True) + EPS) * g + b


def _in_proj_kernel(x_ref, g_ref, w_ref, o_ref):
    x = x_ref[...]
    h = (x * _rms_scale(x) * g_ref[...]).astype(BF16)
    o_ref[...] = jnp.dot(h, w_ref[...], preferred_element_type=F32)


def _in_proj(x2, g, w, tm):
    T = x2.shape[0]
    return pl.pallas_call(
        _in_proj_kernel,
        grid=(T // tm,),
        in_specs=[
            pl.BlockSpec((tm, D_MODEL), lambda i: (i, 0)),
            pl.BlockSpec((1, D_MODEL), lambda i: (0, 0)),
            pl.BlockSpec((D_MODEL, D_IN), lambda i: (0, 0), pipeline_mode=pl.Buffered(1)),
        ],
        out_specs=pl.BlockSpec((tm, D_IN), lambda i: (i, 0)),
        out_shape=jax.ShapeDtypeStruct((T, D_IN), F32),
        compiler_params=pltpu.CompilerParams(
            dimension_semantics=("arbitrary",), vmem_limit_bytes=VMEM_LIMIT),
        name="in_proj",
    )(x2, g, w)


def _head_sumsq(x, ones_bd):
    x2 = x * x
    hi = x2.astype(BF16)
    lo = (x2 - hi.astype(F32)).astype(BF16)
    return (jnp.dot(hi, ones_bd, preferred_element_type=F32)
            + jnp.dot(lo, ones_bd, preferred_element_type=F32))


def _mixers_init(cw_ref, sw_ref, w8_ref, wsg_ref, ones_ref, mask_ref):
    r256 = lax.broadcasted_iota(jnp.int32, (KV_WIDTH, KV_WIDTH), 0)
    c256 = lax.broadcasted_iota(jnp.int32, (KV_WIDTH, KV_WIDTH), 1)
    ones_ref[...] = jnp.where((r256 // HEAD_DIM) == (c256 // HEAD_DIM), 1.0, 0.0).astype(BF16)
    for j in range(CONV_KERNEL):
        w8_ref[j] = jnp.broadcast_to(cw_ref[j:j + 1, :], (SUBLANES, CONV_WIDTH))
    ri = lax.broadcasted_iota(jnp.int32, (2 * CHUNK, CHUNK), 0) % CHUNK
    cj = lax.broadcasted_iota(jnp.int32, (2 * CHUNK, CHUNK), 1)
    for p in range(SGU_HEADS // 2):
        wsg_ref[p] = jnp.where(cj <= ri, sw_ref[p], 0.0).astype(BF16)
    qi = lax.broadcasted_iota(jnp.int32, (WINDOW, 2 * WINDOW), 0)
    sj = lax.broadcasted_iota(jnp.int32, (WINDOW, 2 * WINDOW), 1)
    rel = qi + WINDOW - sj
    mask_ref[1] = jnp.where((rel >= 0) & (rel < WINDOW), 1.0, 0.0)


def _mixers_prelude(t, krep_ref, vwin_ref, tail_ref, mask_ref):
    @pl.when(t == 0)
    def _():
        krep_ref[...] = jnp.zeros_like(krep_ref)
        vwin_ref[...] = jnp.zeros_like(vwin_ref)
        tail_ref[1] = jnp.zeros((CONV_HALO, CONV_WIDTH), F32)

    qi = lax.broadcasted_iota(jnp.int32, (WINDOW, 2 * WINDOW), 0)
    sj = (lax.broadcasted_iota(jnp.int32, (WINDOW, 2 * WINDOW), 1) + WINDOW) % (2 * WINDOW)
    rel = qi + WINDOW - sj
    mask_ref[0] = jnp.where((rel >= 0) & (rel < WINDOW) & (sj >= (1 - jnp.minimum(t, 1)) * WINDOW),
                            1.0, 0.0)


def _exact_zero(v):
    bits = lax.bitcast_convert_type(v, jnp.int32)
    return lax.shift_right_logical(lax.shift_right_logical(bits, 16), 16).astype(F32)


def _low_half():
    return lax.broadcasted_iota(jnp.int32, (WINDOW, LANES), 1) < HALF


def _attention_pieces(parity, q_ref, k_ref, v_ref, gq_ref, gk_ref, sinks_ref, og_ref, o_ref,
                      krep_ref, vwin_ref, ones_ref, mask_ref):
    low_half = _low_half()
    ones_bd = ones_ref[...]
    colblk = lax.broadcasted_iota(jnp.int32, (WINDOW, KV_WIDTH), 1) // HEAD_DIM
    gq = gq_ref[...] * (HEAD_DIM ** -0.5)
    k = k_ref[...]
    kn = k * lax.rsqrt(_head_sumsq(k, ones_bd) * (1.0 / HEAD_DIM) + EPS) * gk_ref[...]
    cur = pl.ds(pl.multiple_of(parity * WINDOW, WINDOW), WINDOW)
    vwin_ref[cur, :] = v_ref[...].astype(BF16)
    for p in range(N_KV_HEADS // 2):
        blk = kn[:, p * LANES:(p + 1) * LANES]
        swapped = pltpu.roll(blk, HALF, 1)
        even = jnp.where(low_half, blk, swapped).astype(BF16)
        odd = jnp.where(low_half, swapped, blk).astype(BF16)
        krep_ref[2 * p, cur, :] = jnp.concatenate([even, even], axis=1)
        krep_ref[2 * p + 1, cur, :] = jnp.concatenate([odd, odd], axis=1)

    yield
    outs = []
    for kh in range(N_KV_HEADS):
        qs = q_ref[:, kh * KV_WIDTH:(kh + 1) * KV_WIDTH]
        qn = (qs * lax.rsqrt(_head_sumsq(qs, ones_bd) * (1.0 / HEAD_DIM) + EPS)
              * gq[:, kh * KV_WIDTH:(kh + 1) * KV_WIDTH]).astype(BF16)
        a = jnp.concatenate(
            [jnp.where(colblk == g, qn, jnp.zeros_like(qn)) for g in range(GROUP)], axis=0)
        logits = lax.dot_general(a, krep_ref[kh], (((1,), (1,)), ((), ())),
                                 preferred_element_type=F32)
        yield
        probs = []
        for r0 in range(0, GROUP * WINDOW, SOFTMAX_ROWS):
            q0 = r0 % WINDOW
            valid = mask_ref[parity, q0:q0 + SOFTMAX_ROWS, :] > 0.5
            lg = jnp.where(valid, logits[r0:r0 + SOFTMAX_ROWS], NEG_INF)
            sink = sinks_ref[kh * GROUP + r0 // WINDOW]
            m = jnp.maximum(jnp.max(lg, axis=-1, keepdims=True), sink)
            pexp = jnp.exp(lg - m)
            denom = jnp.sum(pexp, axis=-1, keepdims=True) + jnp.exp(sink - m)
            probs.append((pexp * (1.0 / denom)).astype(BF16))
            if (r0 // SOFTMAX_ROWS) % 4 == 3 and r0 + SOFTMAX_ROWS < GROUP * WINDOW:
                yield
        outs.append(jnp.dot(jnp.concatenate(probs, axis=0), vwin_ref[...],
                            preferred_element_type=F32))
    y = jnp.concatenate(
        [jnp.where(low_half,
                   outs[2 * p][g * WINDOW:(g + 1) * WINDOW, p * LANES:(p + 1) * LANES],
                   outs[2 * p + 1][g * WINDOW:(g + 1) * WINDOW, p * LANES:(p + 1) * LANES])
         for p in range(N_KV_HEADS // 2) for g in range(GROUP)], axis=1)
    o_ref[:, 0:ATTN_WIDTH] = (y * _rms_scale(y) * og_ref[:, 0:ATTN_WIDTH]).astype(BF16)


def _conv_pieces(parity, ca_ref, cg_ref, cb_ref, clg_ref, clb_ref, og_ref, o_ref,
                 hbuf_ref, tail_ref, hs_ref, w8_ref, anchors):
    hbuf_ref[0:CONV_HALO, :] = tail_ref[1 - parity]
    hbuf_ref[CONV_HALO:CONV_HALO + BLOCK, :] = ca_ref[...] * jax.nn.sigmoid(cg_ref[...])
    tail_ref[parity] = hbuf_ref[BLOCK:BLOCK + CONV_HALO, :]
    for r in range(1, SUBLANES):
        hs_ref[r - 1] = hbuf_ref[r:r + BLOCK + CONV_HALO - SUBLANES, :]
    groups = CONV_ROWS // SUBLANES
    for r0 in range(0, BLOCK, CONV_ROWS):
        yield
        accs = [None] * groups
        for j in range(CONV_KERNEL):
            if j % CONV_TAPS_PER_PIECE == 0 and j:
                yield
            a, r = divmod(CONV_HALO - (CONV_KERNEL - 1) + j, SUBLANES)
            w8 = w8_ref[j]
            for g in range(groups):
                base = r0 + SUBLANES * (a + g)
                if r == 0:
                    src = hbuf_ref[base:base + SUBLANES, :]
                else:
                    src = hs_ref[r - 1, base:base + SUBLANES, :]
                accs[g] = src * w8 if accs[g] is None else accs[g] + src * w8
        acc = jnp.concatenate(accs, axis=0)
        hc = _layer_norm(acc + cb_ref[...], clg_ref[...], clb_ref[...])
        yc = hc * jax.nn.sigmoid(hc)
        o_ref[r0:r0 + CONV_ROWS, ATTN_WIDTH:ATTN_WIDTH + CONV_WIDTH] = (
            yc * _rms_scale(yc) * og_ref[:, ATTN_WIDTH:ATTN_WIDTH + CONV_WIDTH]).astype(BF16)
        anchors.append(yc[0:2 * SUBLANES, 0:LANES])


def _sgu_pieces(su_ref, sv_ref, slg_ref, slb_ref, sb_ref, og_ref, o_ref, wsg_ref):
    low_half = _low_half()
    vn = _layer_norm(sv_ref[...], slg_ref[...], slb_ref[...]).astype(BF16)
    ys = []
    for p in range(SGU_HEADS // 2):
        r = jnp.dot(wsg_ref[p], vn[:, p * LANES:(p + 1) * LANES],
                    preferred_element_type=F32)
        sp = jnp.where(low_half, r[0:CHUNK], r[CHUNK:2 * CHUNK])
        sp = sp + sb_ref[:, p * LANES:(p + 1) * LANES]
        ys.append(su_ref[:, p * LANES:(p + 1) * LANES] * sp)
    ysg = jnp.concatenate(ys, axis=1)
    o_ref[:, ATTN_WIDTH + CONV_WIDTH:] = (
        ysg * _rms_scale(ysg) * og_ref[:, ATTN_WIDTH + CONV_WIDTH:]).astype(BF16)
    yield


def _layer_tail_kernel(q_ref, k_ref, v_ref, ca_ref, cg_ref, su_ref, sv_ref, x_ref,
                       gq_ref, gk_ref, sinks_ref, cw_ref, cb_ref, clg_ref, clb_ref,
                       slg_ref, slb_ref, sw_ref, sb_ref, og_ref,
                       wo_ref, g2_ref, wu_ref, wd_ref,
                       o_ref,
                       mix_ref, x1_ref, h_ref, uu_ref,
                       krep_ref, vwin_ref, hbuf_ref, tail_ref, hs_ref, w8_ref, wsg_ref, ones_ref,
                       mask_ref, *, n_tiles, blocks_per_tile, blocks_per_seq):
    i = pl.program_id(0)
    f = pl.program_id(1)
    blk = jnp.minimum(i, n_tiles - 1) * blocks_per_tile + f // 2
    rows = pl.ds(pl.multiple_of((f // 2) * BLOCK, BLOCK), BLOCK)
    h_mlp = (i + 1) % 2
    h_new = i % 2

    @pl.when((i == 0) & (f == 0))
    def _():
        _mixers_init(cw_ref, sw_ref, w8_ref, wsg_ref, ones_ref, mask_ref)
        x1_ref[...] = jnp.zeros_like(x1_ref)
        h_ref[...] = jnp.zeros_like(h_ref)

    @pl.when(f == 0)
    def _():
        o_ref[...] = x1_ref[...]

    def mlp_chunk(anchors=()):
        tf = wu_ref.shape[1]
        kh = D_MODEL // 2
        for n in range(0, tf, MXU_TILE):
            u = jnp.dot(h_ref[h_mlp, :, 0:kh], wu_ref[0:kh, n:n + MXU_TILE],
                        preferred_element_type=F32)
            yield
            u = u + jnp.dot(h_ref[h_mlp, :, kh:], wu_ref[kh:, n:n + MXU_TILE],
                            preferred_element_type=F32)
            u = jnp.maximum(u, 0.0)
            uu_ref[:, n:n + MXU_TILE] = (u * u).astype(BF16)
            yield
        for m in range(0, D_MODEL, MXU_TILE):
            while anchors:
                uu_ref[0:2 * SUBLANES, 0:LANES] += _exact_zero(anchors.pop()).astype(BF16)
            o_ref[:, m:m + MXU_TILE] += jnp.dot(uu_ref[...], wd_ref[:, m:m + MXU_TILE],
                                                preferred_element_type=F32)
            yield

    def interleave(*streams):
        done = [0] * len(streams)
        live = set(range(len(streams)))
        while live:
            s = min(live, key=lambda j: (done[j] + 1) / streams[j][1])
            done[s] += 1
            if next(streams[s][0], _DONE) is _DONE:
                live.remove(s)

    @pl.when(f % 2 == 0)
    def _():
        t = blk % blocks_per_seq
        parity = t % 2
        _mixers_prelude(t, krep_ref, vwin_ref, tail_ref, mask_ref)
        anchors = []
        interleave(
            (_conv_pieces(parity, ca_ref, cg_ref, cb_ref, clg_ref, clb_ref, og_ref, mix_ref,
                          hbuf_ref, tail_ref, hs_ref, w8_ref, anchors), CONV_PIECES),
            (_attention_pieces(parity, q_ref, k_ref, v_ref, gq_ref, gk_ref, sinks_ref, og_ref,
                               mix_ref, krep_ref, vwin_ref, ones_ref, mask_ref), ATTENTION_PIECES),
            (_sgu_pieces(su_ref, sv_ref, slg_ref, slb_ref, sb_ref, og_ref, mix_ref, wsg_ref), 1),
            (mlp_chunk(anchors), MLP_PIECES))

    @pl.when(f % 2 == 1)
    def _():
        x1 = x_ref[...] + jnp.dot(mix_ref[...], wo_ref[...], preferred_element_type=F32)
        x1_ref[rows, :] = x1
        h_ref[h_new, rows, :] = (x1 * _rms_scale(x1) * g2_ref[...]).astype(BF16)
        interleave((mlp_chunk(), 1))


def _layer_tail(proj, x2, B, S, tm, tf, gq, gk, sinks, cw, cb, clg, clb, slg, slb, sw, sb, og,
                wo, g2, wu, wd):
    T = B * S
    n_tiles = T // tm
    bpt = tm // BLOCK
    n_f = D_FF // tf
    assert n_f == 2 * bpt, "each mixer block takes two MLP chunk steps"
    assert S % BLOCK == 0 and T % tm == 0

    def blk(i, f):
        return jnp.minimum(i, n_tiles - 1) * bpt + f // 2

    def col(width, c):
        return pl.BlockSpec((BLOCK, width), lambda i, f: (blk(i, f), c))

    def whole(shape, **kw):
        return pl.BlockSpec(shape, lambda i, f: (0,) * len(shape), **kw)

    return pl.pallas_call(
        functools.partial(_layer_tail_kernel, n_tiles=n_tiles, blocks_per_tile=bpt,
                          blocks_per_seq=S // BLOCK),
        grid=(n_tiles + 1, n_f),
        in_specs=[
            col(ATTN_WIDTH, 0),
            col(KV_WIDTH, 4),
            col(KV_WIDTH, 5),
            col(CONV_WIDTH, 3),
            col(CONV_WIDTH, 4),
            col(SGU_WIDTH, 5),
            col(SGU_WIDTH, 6),
            col(D_MODEL, 0),
            whole((1, ATTN_WIDTH)), whole((1, KV_WIDTH)),
            pl.BlockSpec(memory_space=pltpu.MemorySpace.SMEM),
            whole((CONV_KERNEL, CONV_WIDTH)), whole((1, CONV_WIDTH)),
            whole((1, CONV_WIDTH)), whole((1, CONV_WIDTH)),
            whole((1, SGU_WIDTH)), whole((1, SGU_WIDTH)),
            whole((SGU_HEADS // 2, 2 * CHUNK, CHUNK)), whole((CHUNK, SGU_WIDTH)),
            whole((1, D_MODEL)),
            whole((D_MODEL, D_MODEL), pipeline_mode=pl.Buffered(1)),
            whole((1, D_MODEL)),
            pl.BlockSpec((D_MODEL, tf), lambda i, f: (0, f)),
            pl.BlockSpec((tf, D_MODEL), lambda i, f: (f, 0)),
        ],
        out_specs=pl.BlockSpec((tm, D_MODEL), lambda i, f: (jnp.maximum(i - 1, 0), 0)),
        out_shape=jax.ShapeDtypeStruct((T, D_MODEL), F32),
        scratch_shapes=[
            pltpu.VMEM((BLOCK, D_MODEL), BF16),
            pltpu.VMEM((tm, D_MODEL), F32),
            pltpu.VMEM((2, tm, D_MODEL), BF16),
            pltpu.VMEM((tm, tf), BF16),
            pltpu.VMEM((N_KV_HEADS, 2 * WINDOW, KV_WIDTH), BF16),
            pltpu.VMEM((2 * WINDOW, KV_WIDTH), BF16),
            pltpu.VMEM((CONV_HALO + BLOCK, CONV_WIDTH), F32),
            pltpu.VMEM((2, CONV_HALO, CONV_WIDTH), F32),
            pltpu.VMEM((SUBLANES - 1, CONV_HALO + BLOCK - SUBLANES, CONV_WIDTH), F32),
            pltpu.VMEM((CONV_KERNEL, SUBLANES, CONV_WIDTH), F32),
            pltpu.VMEM((SGU_HEADS // 2, 2 * CHUNK, CHUNK), BF16),
            pltpu.VMEM((KV_WIDTH, KV_WIDTH), BF16),
            pltpu.VMEM((2, WINDOW, 2 * WINDOW), F32),
        ],
        compiler_params=pltpu.CompilerParams(
            dimension_semantics=("arbitrary", "arbitrary"), vmem_limit_bytes=VMEM_LIMIT),
        name="layer_tail",
    )(proj, proj, proj, proj, proj, proj, proj, x2,
      gq, gk, sinks, cw, cb, clg, clb, slg, slb, sw, sb, og, wo, g2, wu, wd)


def _attn_perm():
    perm = np.empty((ATTN_WIDTH,), np.int32)
    for p in range(N_KV_HEADS // 2):
        for g in range(GROUP):
            for e in range(2):
                head = GROUP * (2 * p + e) + g
                dst = (GROUP * p + g) * LANES + e * HALF
                perm[dst:dst + HEAD_DIM] = np.arange(head * HEAD_DIM, (head + 1) * HEAD_DIM)
    return perm


def _pick_tile(n, pref):
    while n % pref:
        pref //= 2
    return pref


def kernel(x, ln1_g, w_in, q_norm_g, k_norm_g, sinks, conv_w, conv_b, conv_ln_g, conv_ln_b,
           sgu_ln_g, sgu_ln_b, sgu_w, sgu_b, out_norm_g, w_out, ln2_g, w_up, w_down):
    B, S, D = x.shape
    depth = w_in.shape[0]
    assert D == D_MODEL and S % WINDOW == 0
    T = B * S
    tm_proj = _pick_tile(T, 512)
    tm_tail = _pick_tile(T, 512)
    tf = D_FF // (2 * (tm_tail // BLOCK))
    perm = np.concatenate([_attn_perm(), np.arange(ATTN_WIDTH, D_MODEL, dtype=np.int32)])

    x2 = x.reshape(T, D)
    for l in range(depth):
        proj = _in_proj(x2, ln1_g[l][None, :], w_in[l].astype(BF16), tm_proj)
        x2 = _layer_tail(
            proj, x2, B, S, tm_tail, tf,
            jnp.tile(q_norm_g[l], N_Q_HEADS)[None, :], jnp.tile(k_norm_g[l], N_KV_HEADS)[None, :],
            sinks[l], conv_w[l], conv_b[l][None, :], conv_ln_g[l][None, :], conv_ln_b[l][None, :],
            sgu_ln_g[l][None, :], sgu_ln_b[l][None, :],
            sgu_w[l].reshape(SGU_HEADS // 2, 2 * CHUNK, CHUNK),
            jnp.repeat(sgu_b[l].T, HEAD_DIM, axis=1),
            out_norm_g[l][perm][None, :],
            w_out[l][perm, :].astype(BF16), ln2_g[l][None, :],
            w_up[l].astype(BF16), w_down[l].astype(BF16))
    return x2.reshape(B, S, D)
```

```python
import functools

import jax
import jax.numpy as jnp
import numpy as np
from jax import lax
from jax.experimental import pallas as pl
from jax.experimental.pallas import tpu as pltpu

F32 = jnp.float32
BF16 = jnp.bfloat16

D_MODEL = 2048
HEAD_DIM = 64
ATTN_WIDTH = 1024
N_Q_HEADS = 16
N_KV_HEADS = 4
GROUP = N_Q_HEADS // N_KV_HEADS
KV_WIDTH = 256
WINDOW = 128
CONV_WIDTH = 512
CONV_KERNEL = 31
CONV_HALO = 32
SGU_WIDTH = 512
SGU_HEADS = 8
CHUNK = 128
D_FF = 4 * D_MODEL
D_IN = ATTN_WIDTH + 2 * KV_WIDTH + 2 * CONV_WIDTH + 2 * SGU_WIDTH
EPS = 1e-6
NEG_INF = -1e30
LANES = 128
HALF = LANES // 2
SUBLANES = 8
SOFTMAX_ROWS = 64
CONV_ROWS = 32
BLOCK = WINDOW
MXU_TILE = 256
_DONE = object()
CONV_TAPS_PER_PIECE = 8
ATTENTION_PIECES = 2 + 2 * N_KV_HEADS
CONV_PIECES = 1 + (BLOCK // CONV_ROWS) * -(-CONV_KERNEL // CONV_TAPS_PER_PIECE)
MLP_PIECES = 16
OUT_PROJ_PIECES = 2 * (D_MODEL // MXU_TILE)
SPLIT_K = ATTN_WIDTH
SPLIT_V = SPLIT_K + KV_WIDTH
SPLIT_CONV = SPLIT_V + KV_WIDTH
SPLIT_SGU = SPLIT_CONV + 2 * CONV_WIDTH

VMEM_LIMIT = 60 * 1024 * 1024


def _rms_scale(x):
    return lax.rsqrt(jnp.mean(x * x, axis=-1, keepdims=True) + EPS)


def _layer_norm(x, g, b):
    mu = jnp.mean(x, axis=-1, keepdims=True)
    xc = x - mu
    return xc * lax.rsqrt(jnp.mean(xc * xc, axis=-1, keepdims=True) + EPS) * g + b


def _in_proj_kernel(x_ref, g_ref, w_ref, o_ref):
    x = x_ref[...]
    h = (x * _rms_scale(x) * g_ref[...]).astype(BF16)
    o_ref[...] = jnp.dot(h, w_ref[...], preferred_element_type=F32)


def _in_proj(x2, g, w, tm):
    T = x2.shape[0]
    return pl.pallas_call(
        _in_proj_kernel,
        grid=(T // tm,),
        in_specs=[
            pl.BlockSpec((tm, D_MODEL), lambda i: (i, 0)),
            pl.BlockSpec((1, D_MODEL), lambda i: (0, 0)),
            pl.BlockSpec((D_MODEL, D_IN), lambda i: (0, 0), pipeline_mode=pl.Buffered(1)),
        ],
        out_specs=pl.BlockSpec((tm, D_IN), lambda i: (i, 0)),
        out_shape=jax.ShapeDtypeStruct((T, D_IN), F32),
        compiler_params=pltpu.CompilerParams(
            dimension_semantics=("arbitrary",), vmem_limit_bytes=VMEM_LIMIT),
        name="in_proj",
    )(x2, g, w)


def _head_sumsq(x, ones_bd):
    x2 = x * x
    hi = x2.astype(BF16)
    lo = (x2 - hi.astype(F32)).astype(BF16)
    return (jnp.dot(hi, ones_bd, preferred_element_type=F32)
            + jnp.dot(lo, ones_bd, preferred_element_type=F32))


def _mixers_init(cw_ref, sw_ref, w8_ref, wsg_ref, ones_ref, mask_ref):
    r256 = lax.broadcasted_iota(jnp.int32, (KV_WIDTH, KV_WIDTH), 0)
    c256 = lax.broadcasted_iota(jnp.int32, (KV_WIDTH, KV_WIDTH), 1)
    ones_ref[...] = jnp.where((r256 // HEAD_DIM) == (c256 // HEAD_DIM), 1.0, 0.0).astype(BF16)
    for j in range(CONV_KERNEL):
        w8_ref[j] = jnp.broadcast_to(cw_ref[j:j + 1, :], (SUBLANES, CONV_WIDTH))
    ri = lax.broadcasted_iota(jnp.int32, (2 * CHUNK, CHUNK), 0) % CHUNK
    cj = lax.broadcasted_iota(jnp.int32, (2 * CHUNK, CHUNK), 1)
    for p in range(SGU_HEADS // 2):
        wsg_ref[p] = jnp.where(cj <= ri, sw_ref[p], 0.0).astype(BF16)
    qi = lax.broadcasted_iota(jnp.int32, (WINDOW, 2 * WINDOW), 0)
    sj = lax.broadcasted_iota(jnp.int32, (WINDOW, 2 * WINDOW), 1)
    rel = qi + WINDOW - sj
    mask_ref[1] = jnp.where((rel >= 0) & (rel < WINDOW), 1.0, 0.0)


def _mixers_prelude(t, krep_ref, vwin_ref, tail_ref, mask_ref):
    @pl.when(t == 0)
    def _():
        krep_ref[...] = jnp.zeros_like(krep_ref)
        vwin_ref[...] = jnp.zeros_like(vwin_ref)
        tail_ref[1] = jnp.zeros((CONV_HALO, CONV_WIDTH), F32)

    qi = lax.broadcasted_iota(jnp.int32, (WINDOW, 2 * WINDOW), 0)
    sj = (lax.broadcasted_iota(jnp.int32, (WINDOW, 2 * WINDOW), 1) + WINDOW) % (2 * WINDOW)
    rel = qi + WINDOW - sj
    mask_ref[0] = jnp.where((rel >= 0) & (rel < WINDOW) & (sj >= (1 - jnp.minimum(t, 1)) * WINDOW),
                            1.0, 0.0)


def _exact_zero(v):
    bits = lax.bitcast_convert_type(v, jnp.int32)
    return lax.shift_right_logical(lax.shift_right_logical(bits, 16), 16).astype(F32)


def _low_half():
    return lax.broadcasted_iota(jnp.int32, (WINDOW, LANES), 1) < HALF


def _attention_pieces(parity, q_ref, k_ref, v_ref, gq_ref, gk_ref, sinks_ref, og_ref, o_ref,
                      krep_ref, vwin_ref, ones_ref, mask_ref):
    low_half = _low_half()
    ones_bd = ones_ref[...]
    colblk = lax.broadcasted_iota(jnp.int32, (WINDOW, KV_WIDTH), 1) // HEAD_DIM
    gq = gq_ref[...] * (HEAD_DIM ** -0.5)
    k = k_ref[...]
    kn = k * lax.rsqrt(_head_sumsq(k, ones_bd) * (1.0 / HEAD_DIM) + EPS) * gk_ref[...]
    cur = pl.ds(pl.multiple_of(parity * WINDOW, WINDOW), WINDOW)
    vwin_ref[cur, :] = v_ref[...].astype(BF16)
    for p in range(N_KV_HEADS // 2):
        blk = kn[:, p * LANES:(p + 1) * LANES]
        swapped = pltpu.roll(blk, HALF, 1)
        even = jnp.where(low_half, blk, swapped).astype(BF16)
        odd = jnp.where(low_half, swapped, blk).astype(BF16)
        krep_ref[2 * p, cur, :] = jnp.concatenate([even, even], axis=1)
        krep_ref[2 * p + 1, cur, :] = jnp.concatenate([odd, odd], axis=1)

    yield
    outs = []
    for kh in range(N_KV_HEADS):
        qs = q_ref[:, kh * KV_WIDTH:(kh + 1) * KV_WIDTH]
        qn = (qs * lax.rsqrt(_head_sumsq(qs, ones_bd) * (1.0 / HEAD_DIM) + EPS)
              * gq[:, kh * KV_WIDTH:(kh + 1) * KV_WIDTH]).astype(BF16)
        a = jnp.concatenate(
            [jnp.where(colblk == g, qn, jnp.zeros_like(qn)) for g in range(GROUP)], axis=0)
        logits = lax.dot_general(a, krep_ref[kh], (((1,), (1,)), ((), ())),
                                 preferred_element_type=F32)
        yield
        probs = []
        for r0 in range(0, GROUP * WINDOW, SOFTMAX_ROWS):
            q0 = r0 % WINDOW
            valid = mask_ref[parity, q0:q0 + SOFTMAX_ROWS, :] > 0.5
            lg = jnp.where(valid, logits[r0:r0 + SOFTMAX_ROWS], NEG_INF)
            sink = sinks_ref[kh * GROUP + r0 // WINDOW]
            m = jnp.maximum(jnp.max(lg, axis=-1, keepdims=True), sink)
            pexp = jnp.exp(lg - m)
            denom = jnp.sum(pexp, axis=-1, keepdims=True) + jnp.exp(sink - m)
            probs.append((pexp * (1.0 / denom)).astype(BF16))
            if (r0 // SOFTMAX_ROWS) % 4 == 3 and r0 + SOFTMAX_ROWS < GROUP * WINDOW:
                yield
        outs.append(jnp.dot(jnp.concatenate(probs, axis=0), vwin_ref[...],
                            preferred_element_type=F32))
    y = jnp.concatenate(
        [jnp.where(low_half,
                   outs[2 * p][g * WINDOW:(g + 1) * WINDOW, p * LANES:(p + 1) * LANES],
                   outs[2 * p + 1][g * WINDOW:(g + 1) * WINDOW, p * LANES:(p + 1) * LANES])
         for p in range(N_KV_HEADS // 2) for g in range(GROUP)], axis=1)
    o_ref[:, 0:ATTN_WIDTH] = (y * _rms_scale(y) * og_ref[:, 0:ATTN_WIDTH]).astype(BF16)


def _conv_pieces(parity, ca_ref, cg_ref, cb_ref, clg_ref, clb_ref, og_ref, o_ref,
                 hbuf_ref, tail_ref, hs_ref, w8_ref, anchors):
    hbuf_ref[0:CONV_HALO, :] = tail_ref[1 - parity]
    hbuf_ref[CONV_HALO:CONV_HALO + BLOCK, :] = ca_ref[...] * jax.nn.sigmoid(cg_ref[...])
    tail_ref[parity] = hbuf_ref[BLOCK:BLOCK + CONV_HALO, :]
    for r in range(1, SUBLANES):
        hs_ref[r - 1] = hbuf_ref[r:r + BLOCK + CONV_HALO - SUBLANES, :]
    groups = CONV_ROWS // SUBLANES
    for r0 in range(0, BLOCK, CONV_ROWS):
        yield
        accs = [None] * groups
        for j in range(CONV_KERNEL):
            if j % CONV_TAPS_PER_PIECE == 0 and j:
                yield
            a, r = divmod(CONV_HALO - (CONV_KERNEL - 1) + j, SUBLANES)
            w8 = w8_ref[j]
            for g in range(groups):
                base = r0 + SUBLANES * (a + g)
                if r == 0:
                    src = hbuf_ref[base:base + SUBLANES, :]
                else:
                    src = hs_ref[r - 1, base:base + SUBLANES, :]
                accs[g] = src * w8 if accs[g] is None else accs[g] + src * w8
        acc = jnp.concatenate(accs, axis=0)
        hc = _layer_norm(acc + cb_ref[...], clg_ref[...], clb_ref[...])
        yc = hc * jax.nn.sigmoid(hc)
        o_ref[r0:r0 + CONV_ROWS, ATTN_WIDTH:ATTN_WIDTH + CONV_WIDTH] = (
            yc * _rms_scale(yc) * og_ref[:, ATTN_WIDTH:ATTN_WIDTH + CONV_WIDTH]).astype(BF16)
        anchors.append(yc[0:2 * SUBLANES, 0:LANES])


def _sgu_pieces(su_ref, sv_ref, slg_ref, slb_ref, sb_ref, og_ref, o_ref, wsg_ref):
    low_half = _low_half()
    vn = _layer_norm(sv_ref[...], slg_ref[...], slb_ref[...]).astype(BF16)
    ys = []
    for p in range(SGU_HEADS // 2):
        r = jnp.dot(wsg_ref[p], vn[:, p * LANES:(p + 1) * LANES],
                    preferred_element_type=F32)
        sp = jnp.where(low_half, r[0:CHUNK], r[CHUNK:2 * CHUNK])
        sp = sp + sb_ref[:, p * LANES:(p + 1) * LANES]
        ys.append(su_ref[:, p * LANES:(p + 1) * LANES] * sp)
    ysg = jnp.concatenate(ys, axis=1)
    o_ref[:, ATTN_WIDTH + CONV_WIDTH:] = (
        ysg * _rms_scale(ysg) * og_ref[:, ATTN_WIDTH + CONV_WIDTH:]).astype(BF16)
    yield


def _layer_tail_kernel(proj_ref, x_ref,
                       gq_ref, gk_ref, sinks_ref, cw_ref, cb_ref, clg_ref, clb_ref,
                       slg_ref, slb_ref, sw_ref, sb_ref, og_ref,
                       wo_ref, g2_ref, wu_ref, wd_ref,
                       o_ref,
                       mix_ref, x1_ref, h_ref, uu_ref,
                       krep_ref, vwin_ref, hbuf_ref, tail_ref, hs_ref, w8_ref, wsg_ref, ones_ref,
                       mask_ref, *, n_tiles, blocks_per_tile, blocks_per_seq):
    i = pl.program_id(0)
    f = pl.program_id(1)
    n_f = pl.num_programs(1)
    first = i == 0
    last = i == n_tiles
    even = f % 2 == 0
    blk = jnp.minimum(i, n_tiles - 1) * blocks_per_tile + f // 2
    rows = pl.ds(pl.multiple_of((f // 2) * BLOCK, BLOCK), BLOCK)
    h_mlp = (i + 1) % 2
    h_new = i % 2

    @pl.when(first & (f == 0))
    def _():
        _mixers_init(cw_ref, sw_ref, w8_ref, wsg_ref, ones_ref, mask_ref)

    @pl.when((f == 0) & jnp.logical_not(first))
    def _():
        o_ref[...] = x1_ref[...]

    def mlp_chunk(anchors=()):
        tf = wu_ref.shape[1]
        kh = D_MODEL // 2
        for n in range(0, tf, MXU_TILE):
            u = jnp.dot(h_ref[h_mlp, :, 0:kh], wu_ref[0:kh, n:n + MXU_TILE],
                        preferred_element_type=F32)
            yield
            u = u + jnp.dot(h_ref[h_mlp, :, kh:], wu_ref[kh:, n:n + MXU_TILE],
                            preferred_element_type=F32)
            u = jnp.maximum(u, 0.0)
            uu_ref[:, n:n + MXU_TILE] = (u * u).astype(BF16)
            yield
        for m in range(0, D_MODEL, MXU_TILE):
            while anchors:
                uu_ref[0:2 * SUBLANES, 0:LANES] += _exact_zero(anchors.pop()).astype(BF16)
            o_ref[:, m:m + MXU_TILE] += jnp.dot(uu_ref[...], wd_ref[:, m:m + MXU_TILE],
                                                preferred_element_type=F32)
            yield

    def mixer_streams(anchors):
        t = blk % blocks_per_seq
        parity = t % 2
        _mixers_prelude(t, krep_ref, vwin_ref, tail_ref, mask_ref)
        x1_ref[rows, :] = x_ref[...]
        mix = mix_ref.at[rows]
        col = lambda a, w: proj_ref.at[:, a:a + w]
        q_ref, k_ref, v_ref = col(0, ATTN_WIDTH), col(SPLIT_K, KV_WIDTH), col(SPLIT_V, KV_WIDTH)
        ca_ref, cg_ref = col(SPLIT_CONV, CONV_WIDTH), col(SPLIT_CONV + CONV_WIDTH, CONV_WIDTH)
        su_ref, sv_ref = col(SPLIT_SGU, SGU_WIDTH), col(SPLIT_SGU + SGU_WIDTH, SGU_WIDTH)
        return [
            (_conv_pieces(parity, ca_ref, cg_ref, cb_ref, clg_ref, clb_ref, og_ref, mix,
                          hbuf_ref, tail_ref, hs_ref, w8_ref, anchors), CONV_PIECES),
            (_attention_pieces(parity, q_ref, k_ref, v_ref, gq_ref, gk_ref, sinks_ref, og_ref,
                               mix, krep_ref, vwin_ref, ones_ref, mask_ref), ATTENTION_PIECES),
            (_sgu_pieces(su_ref, sv_ref, slg_ref, slb_ref, sb_ref, og_ref, mix, wsg_ref), 1)]

    def out_proj_pieces():
        ssq = None
        for n in range(0, D_MODEL, MXU_TILE):
            x1 = x1_ref[:, n:n + MXU_TILE] + jnp.dot(mix_ref[...], wo_ref[:, n:n + MXU_TILE],
                                                     preferred_element_type=F32)
            x1_ref[:, n:n + MXU_TILE] = x1
            part = jnp.sum(x1 * x1, axis=-1, keepdims=True)
            ssq = part if ssq is None else ssq + part
            yield
        scale = lax.rsqrt(ssq * (1.0 / D_MODEL) + EPS)
        for n in range(0, D_MODEL, MXU_TILE):
            h_ref[h_new, :, n:n + MXU_TILE] = (
                x1_ref[:, n:n + MXU_TILE] * scale * g2_ref[:, n:n + MXU_TILE]).astype(BF16)
            yield

    def interleave(streams):
        done = [0] * len(streams)
        live = set(range(len(streams)))
        while live:
            s = min(live, key=lambda j: (done[j] + 1) / streams[j][1])
            done[s] += 1
            if next(streams[s][0], _DONE) is _DONE:
                live.remove(s)

    middle = jnp.logical_not(first | last)
    is_proj = f == n_f - 1

    @pl.when(even & middle)
    def _():
        anchors = []
        interleave(mixer_streams(anchors) + [(mlp_chunk(anchors), MLP_PIECES)])

    @pl.when(even & first)
    def _():
        interleave(mixer_streams([]))

    @pl.when(is_proj & middle)
    def _():
        interleave([(out_proj_pieces(), OUT_PROJ_PIECES), (mlp_chunk(), MLP_PIECES)])

    @pl.when(is_proj & first)
    def _():
        interleave([(out_proj_pieces(), OUT_PROJ_PIECES)])

    @pl.when(jnp.logical_not(first) & (last | jnp.logical_not(even | is_proj)))
    def _():
        interleave([(mlp_chunk(), MLP_PIECES)])


def _layer_tail(proj, x2, B, S, tm, tf, gq, gk, sinks, cw, cb, clg, clb, slg, slb, sw, sb, og,
                wo, g2, wu, wd):
    T = B * S
    n_tiles = T // tm
    bpt = tm // BLOCK
    n_f = D_FF // tf
    assert n_f == 2 * bpt, "each mixer block takes two MLP chunk steps"
    assert S % BLOCK == 0 and T % tm == 0

    def blk(i, f):
        return jnp.minimum(i, n_tiles - 1) * bpt + f // 2

    def whole(shape, **kw):
        return pl.BlockSpec(shape, lambda i, f: (0,) * len(shape), **kw)

    return pl.pallas_call(
        functools.partial(_layer_tail_kernel, n_tiles=n_tiles, blocks_per_tile=bpt,
                          blocks_per_seq=S // BLOCK),
        grid=(n_tiles + 1, n_f),
        in_specs=[
            pl.BlockSpec((BLOCK, D_IN), lambda i, f: (blk(i, f), 0)),
            pl.BlockSpec((BLOCK, D_MODEL), lambda i, f: (blk(i, f), 0)),
            whole((1, ATTN_WIDTH)), whole((1, KV_WIDTH)),
            pl.BlockSpec(memory_space=pltpu.MemorySpace.SMEM),
            whole((CONV_KERNEL, CONV_WIDTH)), whole((1, CONV_WIDTH)),
            whole((1, CONV_WIDTH)), whole((1, CONV_WIDTH)),
            whole((1, SGU_WIDTH)), whole((1, SGU_WIDTH)),
            whole((SGU_HEADS // 2, 2 * CHUNK, CHUNK)), whole((CHUNK, SGU_WIDTH)),
            whole((1, D_MODEL)),
            whole((D_MODEL, D_MODEL), pipeline_mode=pl.Buffered(1)),
            whole((1, D_MODEL)),
            pl.BlockSpec((D_MODEL, tf), lambda i, f: (0, f)),
            pl.BlockSpec((tf, D_MODEL), lambda i, f: (f, 0)),
        ],
        out_specs=pl.BlockSpec((tm, D_MODEL), lambda i, f: (jnp.maximum(i - 1, 0), 0)),
        out_shape=jax.ShapeDtypeStruct((T, D_MODEL), F32),
        scratch_shapes=[
            pltpu.VMEM((tm, D_MODEL), BF16),
            pltpu.VMEM((tm, D_MODEL), F32),
            pltpu.VMEM((2, tm, D_MODEL), BF16),
            pltpu.VMEM((tm, tf), BF16),
            pltpu.VMEM((N_KV_HEADS, 2 * WINDOW, KV_WIDTH), BF16),
            pltpu.VMEM((2 * WINDOW, KV_WIDTH), BF16),
            pltpu.VMEM((CONV_HALO + BLOCK, CONV_WIDTH), F32),
            pltpu.VMEM((2, CONV_HALO, CONV_WIDTH), F32),
            pltpu.VMEM((SUBLANES - 1, CONV_HALO + BLOCK - SUBLANES, CONV_WIDTH), F32),
            pltpu.VMEM((CONV_KERNEL, SUBLANES, CONV_WIDTH), F32),
            pltpu.VMEM((SGU_HEADS // 2, 2 * CHUNK, CHUNK), BF16),
            pltpu.VMEM((KV_WIDTH, KV_WIDTH), BF16),
            pltpu.VMEM((2, WINDOW, 2 * WINDOW), F32),
        ],
        compiler_params=pltpu.CompilerParams(
            dimension_semantics=("arbitrary", "arbitrary"), vmem_limit_bytes=VMEM_LIMIT),
        name="layer_tail",
    )(proj, x2, gq, gk, sinks, cw, cb, clg, clb, slg, slb, sw, sb, og, wo, g2, wu, wd)


def _attn_perm():
    perm = np.empty((ATTN_WIDTH,), np.int32)
    for p in range(N_KV_HEADS // 2):
        for g in range(GROUP):
            for e in range(2):
                head = GROUP * (2 * p + e) + g
                dst = (GROUP * p + g) * LANES + e * HALF
                perm[dst:dst + HEAD_DIM] = np.arange(head * HEAD_DIM, (head + 1) * HEAD_DIM)
    return perm


def _pick_tile(n, pref):
    while n % pref:
        pref //= 2
    return pref


def kernel(x, ln1_g, w_in, q_norm_g, k_norm_g, sinks, conv_w, conv_b, conv_ln_g, conv_ln_b,
           sgu_ln_g, sgu_ln_b, sgu_w, sgu_b, out_norm_g, w_out, ln2_g, w_up, w_down):
    B, S, D = x.shape
    depth = w_in.shape[0]
    assert D == D_MODEL and S % WINDOW == 0
    T = B * S
    tm_proj = _pick_tile(T, 512)
    tm_tail = _pick_tile(T, 512)
    tf = D_FF // (2 * (tm_tail // BLOCK))
    perm = np.concatenate([_attn_perm(), np.arange(ATTN_WIDTH, D_MODEL, dtype=np.int32)])

    x2 = x.reshape(T, D)
    for l in range(depth):
        proj = _in_proj(x2, ln1_g[l][None, :], w_in[l].astype(BF16), tm_proj)
        x2 = _layer_tail(
            proj, x2, B, S, tm_tail, tf,
            jnp.tile(q_norm_g[l], N_Q_HEADS)[None, :], jnp.tile(k_norm_g[l], N_KV_HEADS)[None, :],
            sinks[l], conv_w[l], conv_b[l][None, :], conv_ln_g[l][None, :], conv_ln_b[l][None, :],
            sgu_ln_g[l][None, :], sgu_ln_b[l][None, :],
            sgu_w[l].reshape(SGU_HEADS // 2, 2 * CHUNK, CHUNK),
            jnp.repeat(sgu_b[l].T, HEAD_DIM, axis=1),
            out_norm_g[l][perm][None, :],
            w_out[l][perm, :].astype(BF16), ln2_g[l][None, :],
            w_up[l].astype(BF16), w_down[l].astype(BF16))
    return x2.reshape(B, S, D)
```

```python
import functools

import jax
import jax.numpy as jnp
import numpy as np
from jax import lax
from jax.experimental import pallas as pl
from jax.experimental.pallas import tpu as pltpu

F32 = jnp.float32
BF16 = jnp.bfloat16

D_MODEL = 2048
HEAD_DIM = 64
ATTN_WIDTH = 1024
N_Q_HEADS = 16
N_KV_HEADS = 4
GROUP = N_Q_HEADS // N_KV_HEADS
KV_WIDTH = 256
WINDOW = 128
CONV_WIDTH = 512
CONV_KERNEL = 31
CONV_HALO = 32
SGU_WIDTH = 512
SGU_HEADS = 8
CHUNK = 128
D_FF = 4 * D_MODEL
D_IN = ATTN_WIDTH + 2 * KV_WIDTH + 2 * CONV_WIDTH + 2 * SGU_WIDTH
EPS = 1e-6
NEG_INF = -1e30
LANES = 128
HALF = LANES // 2
SUBLANES = 8
SOFTMAX_ROWS = 64
CONV_ROWS = 32
BLOCK = WINDOW
MXU_TILE = 256
_DONE = object()
CONV_TAPS_PER_PIECE = 8
ATTENTION_PIECES = 2 + 2 * N_KV_HEADS
CONV_PIECES = 1 + (BLOCK // CONV_ROWS) * -(-CONV_KERNEL // CONV_TAPS_PER_PIECE)
MLP_PIECES = 16
OUT_PROJ_PIECES = 2 * (D_MODEL // MXU_TILE)
SPLIT_K = ATTN_WIDTH
SPLIT_V = SPLIT_K + KV_WIDTH
SPLIT_CONV = SPLIT_V + KV_WIDTH
SPLIT_SGU = SPLIT_CONV + 2 * CONV_WIDTH

VMEM_LIMIT = 60 * 1024 * 1024


def _rms_scale(x):
    return lax.rsqrt(jnp.mean(x * x, axis=-1, keepdims=True) + EPS)


def _layer_norm(x, g, b):
    mu = jnp.mean(x, axis=-1, keepdims=True)
    xc = x - mu
    return xc * lax.rsqrt(jnp.mean(xc * xc, axis=-1, keepdims=True) + EPS) * g + b


def _cast_kernel(w_ref, o_ref):
    o_ref[...] = w_ref[...].astype(BF16)


def _cast_rows(w, l, block_rows, src_block=lambda r: r, name="cast_rows"):
    _, K, N = w.shape
    return pl.pallas_call(
        _cast_kernel,
        grid=(K // block_rows,),
        in_specs=[pl.BlockSpec((None, block_rows, N), lambda r: (l, src_block(r), 0))],
        out_specs=pl.BlockSpec((block_rows, N), lambda r: (r, 0)),
        out_shape=jax.ShapeDtypeStruct((K, N), BF16),
        compiler_params=pltpu.CompilerParams(
            dimension_semantics=("arbitrary",), vmem_limit_bytes=VMEM_LIMIT),
        name=name,
    )(w)


def _cast_column_chunks(w, l, tf):
    _, K, N = w.shape
    return pl.pallas_call(
        _cast_kernel,
        grid=(N // tf,),
        in_specs=[pl.BlockSpec((None, K, tf), lambda c: (l, 0, c))],
        out_specs=pl.BlockSpec((None, K, tf), lambda c: (c, 0, 0)),
        out_shape=jax.ShapeDtypeStruct((N // tf, K, tf), BF16),
        compiler_params=pltpu.CompilerParams(
            dimension_semantics=("arbitrary",), vmem_limit_bytes=VMEM_LIMIT),
        name="cast_column_chunks",
    )(w)


def _attn_src_head(c):
    permuted = 2 * GROUP * (c // (2 * GROUP)) + GROUP * (c % 2) + (c // 2) % GROUP
    return jnp.where(c < N_Q_HEADS, permuted, c)


def _in_proj_kernel(x_ref, g_ref, w_ref, o_ref):
    x = x_ref[...]
    h = (x * _rms_scale(x) * g_ref[...]).astype(BF16)
    o_ref[...] = jnp.dot(h, w_ref[...], preferred_element_type=F32)


def _in_proj(x2, g, w, tm):
    T = x2.shape[0]
    return pl.pallas_call(
        _in_proj_kernel,
        grid=(T // tm,),
        in_specs=[
            pl.BlockSpec((tm, D_MODEL), lambda i: (i, 0)),
            pl.BlockSpec((1, D_MODEL), lambda i: (0, 0)),
            pl.BlockSpec((D_MODEL, D_IN), lambda i: (0, 0), pipeline_mode=pl.Buffered(1)),
        ],
        out_specs=pl.BlockSpec((tm, D_IN), lambda i: (i, 0)),
        out_shape=jax.ShapeDtypeStruct((T, D_IN), F32),
        compiler_params=pltpu.CompilerParams(
            dimension_semantics=("arbitrary",), vmem_limit_bytes=VMEM_LIMIT),
        name="in_proj",
    )(x2, g, w)


def _head_sumsq(x, ones_bd):
    x2 = x * x
    hi = x2.astype(BF16)
    lo = (x2 - hi.astype(F32)).astype(BF16)
    return (jnp.dot(hi, ones_bd, preferred_element_type=F32)
            + jnp.dot(lo, ones_bd, preferred_element_type=F32))


def _mixers_init(cw_ref, sw_ref, w8_ref, wsg_ref, ones_ref, mask_ref):
    r256 = lax.broadcasted_iota(jnp.int32, (KV_WIDTH, KV_WIDTH), 0)
    c256 = lax.broadcasted_iota(jnp.int32, (KV_WIDTH, KV_WIDTH), 1)
    ones_ref[...] = jnp.where((r256 // HEAD_DIM) == (c256 // HEAD_DIM), 1.0, 0.0).astype(BF16)
    for j in range(CONV_KERNEL):
        w8_ref[j] = jnp.broadcast_to(cw_ref[j:j + 1, :], (SUBLANES, CONV_WIDTH))
    ri = lax.broadcasted_iota(jnp.int32, (2 * CHUNK, CHUNK), 0) % CHUNK
    cj = lax.broadcasted_iota(jnp.int32, (2 * CHUNK, CHUNK), 1)
    for p in range(SGU_HEADS // 2):
        wsg_ref[p] = jnp.where(cj <= ri, sw_ref[p], 0.0).astype(BF16)
    qi = lax.broadcasted_iota(jnp.int32, (WINDOW, 2 * WINDOW), 0)
    sj = lax.broadcasted_iota(jnp.int32, (WINDOW, 2 * WINDOW), 1)
    rel = qi + WINDOW - sj
    mask_ref[1] = jnp.where((rel >= 0) & (rel < WINDOW), 1.0, 0.0)


def _mixers_prelude(t, krep_ref, vwin_ref, tail_ref, mask_ref):
    @pl.when(t == 0)
    def _():
        krep_ref[...] = jnp.zeros_like(krep_ref)
        vwin_ref[...] = jnp.zeros_like(vwin_ref)
        tail_ref[1] = jnp.zeros((CONV_HALO, CONV_WIDTH), F32)

    qi = lax.broadcasted_iota(jnp.int32, (WINDOW, 2 * WINDOW), 0)
    sj = (lax.broadcasted_iota(jnp.int32, (WINDOW, 2 * WINDOW), 1) + WINDOW) % (2 * WINDOW)
    rel = qi + WINDOW - sj
    mask_ref[0] = jnp.where((rel >= 0) & (rel < WINDOW) & (sj >= (1 - jnp.minimum(t, 1)) * WINDOW),
                            1.0, 0.0)


def _exact_zero(v):
    bits = lax.bitcast_convert_type(v, jnp.int32)
    return lax.shift_right_logical(lax.shift_right_logical(bits, 16), 16).astype(F32)


def _low_half():
    return lax.broadcasted_iota(jnp.int32, (WINDOW, LANES), 1) < HALF


def _attention_pieces(parity, q_ref, k_ref, v_ref, gq_ref, gk_ref, sinks_ref, og_ref, o_ref,
                      krep_ref, vwin_ref, ones_ref, mask_ref):
    low_half = _low_half()
    ones_bd = ones_ref[...]
    colblk = lax.broadcasted_iota(jnp.int32, (WINDOW, KV_WIDTH), 1) // HEAD_DIM
    gq = gq_ref[...] * (HEAD_DIM ** -0.5)
    k = k_ref[...]
    kn = k * lax.rsqrt(_head_sumsq(k, ones_bd) * (1.0 / HEAD_DIM) + EPS) * gk_ref[...]
    cur = pl.ds(pl.multiple_of(parity * WINDOW, WINDOW), WINDOW)
    vwin_ref[cur, :] = v_ref[...].astype(BF16)
    for p in range(N_KV_HEADS // 2):
        blk = kn[:, p * LANES:(p + 1) * LANES]
        swapped = pltpu.roll(blk, HALF, 1)
        even = jnp.where(low_half, blk, swapped).astype(BF16)
        odd = jnp.where(low_half, swapped, blk).astype(BF16)
        krep_ref[2 * p, cur, :] = jnp.concatenate([even, even], axis=1)
        krep_ref[2 * p + 1, cur, :] = jnp.concatenate([odd, odd], axis=1)

    yield
    outs = []
    for kh in range(N_KV_HEADS):
        qs = q_ref[:, kh * KV_WIDTH:(kh + 1) * KV_WIDTH]
        qn = (qs * lax.rsqrt(_head_sumsq(qs, ones_bd) * (1.0 / HEAD_DIM) + EPS)
              * gq[:, kh * KV_WIDTH:(kh + 1) * KV_WIDTH]).astype(BF16)
        a = jnp.concatenate(
            [jnp.where(colblk == g, qn, jnp.zeros_like(qn)) for g in range(GROUP)], axis=0)
        logits = lax.dot_general(a, krep_ref[kh], (((1,), (1,)), ((), ())),
                                 preferred_element_type=F32)
        yield
        probs = []
        for r0 in range(0, GROUP * WINDOW, SOFTMAX_ROWS):
            q0 = r0 % WINDOW
            valid = mask_ref[parity, q0:q0 + SOFTMAX_ROWS, :] > 0.5
            lg = jnp.where(valid, logits[r0:r0 + SOFTMAX_ROWS], NEG_INF)
            sink = sinks_ref[kh * GROUP + r0 // WINDOW]
            m = jnp.maximum(jnp.max(lg, axis=-1, keepdims=True), sink)
            pexp = jnp.exp(lg - m)
            denom = jnp.sum(pexp, axis=-1, keepdims=True) + jnp.exp(sink - m)
            probs.append((pexp * (1.0 / denom)).astype(BF16))
            if (r0 // SOFTMAX_ROWS) % 4 == 3 and r0 + SOFTMAX_ROWS < GROUP * WINDOW:
                yield
        outs.append(jnp.dot(jnp.concatenate(probs, axis=0), vwin_ref[...],
                            preferred_element_type=F32))
    y = jnp.concatenate(
        [jnp.where(low_half,
                   outs[2 * p][g * WINDOW:(g + 1) * WINDOW, p * LANES:(p + 1) * LANES],
                   outs[2 * p + 1][g * WINDOW:(g + 1) * WINDOW, p * LANES:(p + 1) * LANES])
         for p in range(N_KV_HEADS // 2) for g in range(GROUP)], axis=1)
    o_ref[:, 0:ATTN_WIDTH] = (y * _rms_scale(y) * og_ref[:, 0:ATTN_WIDTH]).astype(BF16)


def _conv_pieces(parity, ca_ref, cg_ref, cb_ref, clg_ref, clb_ref, og_ref, o_ref,
                 hbuf_ref, tail_ref, hs_ref, w8_ref, anchors):
    hbuf_ref[0:CONV_HALO, :] = tail_ref[1 - parity]
    hbuf_ref[CONV_HALO:CONV_HALO + BLOCK, :] = ca_ref[...] * jax.nn.sigmoid(cg_ref[...])
    tail_ref[parity] = hbuf_ref[BLOCK:BLOCK + CONV_HALO, :]
    for r in range(1, SUBLANES):
        hs_ref[r - 1] = hbuf_ref[r:r + BLOCK + CONV_HALO - SUBLANES, :]
    groups = CONV_ROWS // SUBLANES
    for r0 in range(0, BLOCK, CONV_ROWS):
        yield
        accs = [None] * groups
        for j in range(CONV_KERNEL):
            if j % CONV_TAPS_PER_PIECE == 0 and j:
                yield
            a, r = divmod(CONV_HALO - (CONV_KERNEL - 1) + j, SUBLANES)
            w8 = w8_ref[j]
            for g in range(groups):
                base = r0 + SUBLANES * (a + g)
                if r == 0:
                    src = hbuf_ref[base:base + SUBLANES, :]
                else:
                    src = hs_ref[r - 1, base:base + SUBLANES, :]
                accs[g] = src * w8 if accs[g] is None else accs[g] + src * w8
        acc = jnp.concatenate(accs, axis=0)
        hc = _layer_norm(acc + cb_ref[...], clg_ref[...], clb_ref[...])
        yc = hc * jax.nn.sigmoid(hc)
        o_ref[r0:r0 + CONV_ROWS, ATTN_WIDTH:ATTN_WIDTH + CONV_WIDTH] = (
            yc * _rms_scale(yc) * og_ref[:, ATTN_WIDTH:ATTN_WIDTH + CONV_WIDTH]).astype(BF16)
        anchors.append(yc[0:2 * SUBLANES, 0:LANES])


def _sgu_pieces(su_ref, sv_ref, slg_ref, slb_ref, sb_ref, og_ref, o_ref, wsg_ref):
    low_half = _low_half()
    vn = _layer_norm(sv_ref[...], slg_ref[...], slb_ref[...]).astype(BF16)
    ys = []
    for p in range(SGU_HEADS // 2):
        r = jnp.dot(wsg_ref[p], vn[:, p * LANES:(p + 1) * LANES],
                    preferred_element_type=F32)
        sp = jnp.where(low_half, r[0:CHUNK], r[CHUNK:2 * CHUNK])
        sp = sp + sb_ref[:, p * LANES:(p + 1) * LANES]
        ys.append(su_ref[:, p * LANES:(p + 1) * LANES] * sp)
    ysg = jnp.concatenate(ys, axis=1)
    o_ref[:, ATTN_WIDTH + CONV_WIDTH:] = (
        ysg * _rms_scale(ysg) * og_ref[:, ATTN_WIDTH + CONV_WIDTH:]).astype(BF16)
    yield


def _layer_tail_kernel(proj_ref, x_ref,
                       gq_ref, gk_ref, sinks_ref, cw_ref, cb_ref, clg_ref, clb_ref,
                       slg_ref, slb_ref, sw_ref, sb_ref, og_ref,
                       wo_ref, g2_ref, wu_ref, wd_ref,
                       o_ref,
                       mix_ref, x1_ref, h_ref, uu_ref,
                       krep_ref, vwin_ref, hbuf_ref, tail_ref, hs_ref, w8_ref, wsg_ref, ones_ref,
                       mask_ref, *, n_tiles, blocks_per_tile, blocks_per_seq):
    i = pl.program_id(0)
    f = pl.program_id(1)
    n_f = pl.num_programs(1)
    first = i == 0
    last = i == n_tiles
    even = f % 2 == 0
    blk = jnp.minimum(i, n_tiles - 1) * blocks_per_tile + f // 2
    rows = pl.ds(pl.multiple_of((f // 2) * BLOCK, BLOCK), BLOCK)
    h_mlp = (i + 1) % 2
    h_new = i % 2

    @pl.when(first & (f == 0))
    def _():
        _mixers_init(cw_ref, sw_ref, w8_ref, wsg_ref, ones_ref, mask_ref)

    @pl.when((f == 0) & jnp.logical_not(first))
    def _():
        o_ref[...] = x1_ref[...]

    def mlp_chunk(anchors=()):
        tf = wu_ref.shape[1]
        kh = D_MODEL // 2
        for n in range(0, tf, MXU_TILE):
            u = jnp.dot(h_ref[h_mlp, :, 0:kh], wu_ref[0:kh, n:n + MXU_TILE],
                        preferred_element_type=F32)
            yield
            u = u + jnp.dot(h_ref[h_mlp, :, kh:], wu_ref[kh:, n:n + MXU_TILE],
                            preferred_element_type=F32)
            u = jnp.maximum(u, 0.0)
            uu_ref[:, n:n + MXU_TILE] = (u * u).astype(BF16)
            yield
        for m in range(0, D_MODEL, MXU_TILE):
            while anchors:
                uu_ref[0:2 * SUBLANES, 0:LANES] += _exact_zero(anchors.pop()).astype(BF16)
            o_ref[:, m:m + MXU_TILE] += jnp.dot(uu_ref[...], wd_ref[:, m:m + MXU_TILE],
                                                preferred_element_type=F32)
            yield

    def mixer_streams(anchors):
        t = blk % blocks_per_seq
        parity = t % 2
        _mixers_prelude(t, krep_ref, vwin_ref, tail_ref, mask_ref)
        x1_ref[rows, :] = x_ref[...]
        mix = mix_ref.at[rows]
        col = lambda a, w: proj_ref.at[:, a:a + w]
        q_ref, k_ref, v_ref = col(0, ATTN_WIDTH), col(SPLIT_K, KV_WIDTH), col(SPLIT_V, KV_WIDTH)
        ca_ref, cg_ref = col(SPLIT_CONV, CONV_WIDTH), col(SPLIT_CONV + CONV_WIDTH, CONV_WIDTH)
        su_ref, sv_ref = col(SPLIT_SGU, SGU_WIDTH), col(SPLIT_SGU + SGU_WIDTH, SGU_WIDTH)
        return [
            (_conv_pieces(parity, ca_ref, cg_ref, cb_ref, clg_ref, clb_ref, og_ref, mix,
                          hbuf_ref, tail_ref, hs_ref, w8_ref, anchors), CONV_PIECES),
            (_attention_pieces(parity, q_ref, k_ref, v_ref, gq_ref, gk_ref, sinks_ref, og_ref,
                               mix, krep_ref, vwin_ref, ones_ref, mask_ref), ATTENTION_PIECES),
            (_sgu_pieces(su_ref, sv_ref, slg_ref, slb_ref, sb_ref, og_ref, mix, wsg_ref), 1)]

    def out_proj_pieces():
        ssq = None
        for n in range(0, D_MODEL, MXU_TILE):
            x1 = x1_ref[:, n:n + MXU_TILE] + jnp.dot(mix_ref[...], wo_ref[:, n:n + MXU_TILE],
                                                     preferred_element_type=F32)
            x1_ref[:, n:n + MXU_TILE] = x1
            part = jnp.sum(x1 * x1, axis=-1, keepdims=True)
            ssq = part if ssq is None else ssq + part
            yield
        scale = lax.rsqrt(ssq * (1.0 / D_MODEL) + EPS)
        for n in range(0, D_MODEL, MXU_TILE):
            h_ref[h_new, :, n:n + MXU_TILE] = (
                x1_ref[:, n:n + MXU_TILE] * scale * g2_ref[:, n:n + MXU_TILE]).astype(BF16)
            yield

    def interleave(streams):
        done = [0] * len(streams)
        live = set(range(len(streams)))
        while live:
            s = min(live, key=lambda j: (done[j] + 1) / streams[j][1])
            done[s] += 1
            if next(streams[s][0], _DONE) is _DONE:
                live.remove(s)

    middle = jnp.logical_not(first | last)
    is_proj = f == n_f - 1

    @pl.when(even & middle)
    def _():
        anchors = []
        interleave(mixer_streams(anchors) + [(mlp_chunk(anchors), MLP_PIECES)])

    @pl.when(even & first)
    def _():
        interleave(mixer_streams([]))

    @pl.when(is_proj & middle)
    def _():
        interleave([(out_proj_pieces(), OUT_PROJ_PIECES), (mlp_chunk(), MLP_PIECES)])

    @pl.when(is_proj & first)
    def _():
        interleave([(out_proj_pieces(), OUT_PROJ_PIECES)])

    @pl.when(jnp.logical_not(first) & (last | jnp.logical_not(even | is_proj)))
    def _():
        interleave([(mlp_chunk(), MLP_PIECES)])


def _layer_tail(proj, x2, B, S, tm, tf, gq, gk, sinks, cw, cb, clg, clb, slg, slb, sw, sb, og,
                wo, g2, wu, wd):
    T = B * S
    n_tiles = T // tm
    bpt = tm // BLOCK
    n_f = D_FF // tf
    assert n_f == 2 * bpt, "each mixer block takes two MLP chunk steps"
    assert S % BLOCK == 0 and T % tm == 0

    def blk(i, f):
        return jnp.minimum(i, n_tiles - 1) * bpt + f // 2

    def whole(shape, **kw):
        return pl.BlockSpec(shape, lambda i, f: (0,) * len(shape), **kw)

    return pl.pallas_call(
        functools.partial(_layer_tail_kernel, n_tiles=n_tiles, blocks_per_tile=bpt,
                          blocks_per_seq=S // BLOCK),
        grid=(n_tiles + 1, n_f),
        in_specs=[
            pl.BlockSpec((BLOCK, D_IN), lambda i, f: (blk(i, f), 0)),
            pl.BlockSpec((BLOCK, D_MODEL), lambda i, f: (blk(i, f), 0)),
            whole((1, ATTN_WIDTH)), whole((1, KV_WIDTH)),
            pl.BlockSpec(memory_space=pltpu.MemorySpace.SMEM),
            whole((CONV_KERNEL, CONV_WIDTH)), whole((1, CONV_WIDTH)),
            whole((1, CONV_WIDTH)), whole((1, CONV_WIDTH)),
            whole((1, SGU_WIDTH)), whole((1, SGU_WIDTH)),
            whole((SGU_HEADS // 2, 2 * CHUNK, CHUNK)), whole((CHUNK, SGU_WIDTH)),
            whole((1, D_MODEL)),
            whole((D_MODEL, D_MODEL), pipeline_mode=pl.Buffered(1)),
            whole((1, D_MODEL)),
            pl.BlockSpec((None, D_MODEL, tf), lambda i, f: (f, 0, 0)),
            pl.BlockSpec((tf, D_MODEL), lambda i, f: (f, 0)),
        ],
        out_specs=pl.BlockSpec((tm, D_MODEL), lambda i, f: (jnp.maximum(i - 1, 0), 0)),
        out_shape=jax.ShapeDtypeStruct((T, D_MODEL), F32),
        scratch_shapes=[
            pltpu.VMEM((tm, D_MODEL), BF16),
            pltpu.VMEM((tm, D_MODEL), F32),
            pltpu.VMEM((2, tm, D_MODEL), BF16),
            pltpu.VMEM((tm, tf), BF16),
            pltpu.VMEM((N_KV_HEADS, 2 * WINDOW, KV_WIDTH), BF16),
            pltpu.VMEM((2 * WINDOW, KV_WIDTH), BF16),
            pltpu.VMEM((CONV_HALO + BLOCK, CONV_WIDTH), F32),
            pltpu.VMEM((2, CONV_HALO, CONV_WIDTH), F32),
            pltpu.VMEM((SUBLANES - 1, CONV_HALO + BLOCK - SUBLANES, CONV_WIDTH), F32),
            pltpu.VMEM((CONV_KERNEL, SUBLANES, CONV_WIDTH), F32),
            pltpu.VMEM((SGU_HEADS // 2, 2 * CHUNK, CHUNK), BF16),
            pltpu.VMEM((KV_WIDTH, KV_WIDTH), BF16),
            pltpu.VMEM((2, WINDOW, 2 * WINDOW), F32),
        ],
        compiler_params=pltpu.CompilerParams(
            dimension_semantics=("arbitrary", "arbitrary"), vmem_limit_bytes=VMEM_LIMIT),
        name="layer_tail",
    )(proj, x2, gq, gk, sinks, cw, cb, clg, clb, slg, slb, sw, sb, og, wo, g2, wu, wd)


def _attn_perm():
    perm = np.empty((ATTN_WIDTH,), np.int32)
    for p in range(N_KV_HEADS // 2):
        for g in range(GROUP):
            for e in range(2):
                head = GROUP * (2 * p + e) + g
                dst = (GROUP * p + g) * LANES + e * HALF
                perm[dst:dst + HEAD_DIM] = np.arange(head * HEAD_DIM, (head + 1) * HEAD_DIM)
    return perm


def _pick_tile(n, pref):
    while n % pref:
        pref //= 2
    return pref


def kernel(x, ln1_g, w_in, q_norm_g, k_norm_g, sinks, conv_w, conv_b, conv_ln_g, conv_ln_b,
           sgu_ln_g, sgu_ln_b, sgu_w, sgu_b, out_norm_g, w_out, ln2_g, w_up, w_down):
    B, S, D = x.shape
    depth = w_in.shape[0]
    assert D == D_MODEL and S % WINDOW == 0
    T = B * S
    tm_proj = _pick_tile(T, 512)
    tm_tail = _pick_tile(T, 512)
    tf = D_FF // (2 * (tm_tail // BLOCK))
    perm = np.concatenate([_attn_perm(), np.arange(ATTN_WIDTH, D_MODEL, dtype=np.int32)])

    x2 = x.reshape(T, D)
    for l in range(depth):
        proj = _in_proj(x2, ln1_g[l][None, :], _cast_rows(w_in, l, 256, name="cast_w_in"), tm_proj)
        x2 = _layer_tail(
            proj, x2, B, S, tm_tail, tf,
            jnp.tile(q_norm_g[l], N_Q_HEADS)[None, :], jnp.tile(k_norm_g[l], N_KV_HEADS)[None, :],
            sinks[l], conv_w[l], conv_b[l][None, :], conv_ln_g[l][None, :], conv_ln_b[l][None, :],
            sgu_ln_g[l][None, :], sgu_ln_b[l][None, :],
            sgu_w[l].reshape(SGU_HEADS // 2, 2 * CHUNK, CHUNK),
            jnp.repeat(sgu_b[l].T, HEAD_DIM, axis=1),
            out_norm_g[l][perm][None, :],
            _cast_rows(w_out, l, HEAD_DIM, _attn_src_head, name="cast_w_out"), ln2_g[l][None, :],
            _cast_column_chunks(w_up, l, tf), _cast_rows(w_down, l, 512, name="cast_w_down"))
    return x2.reshape(B, S, D)
```

```python
import functools

import jax
import jax.numpy as jnp
import numpy as np
from jax import lax
from jax.experimental import pallas as pl
from jax.experimental.pallas import tpu as pltpu

F32 = jnp.float32
BF16 = jnp.bfloat16

D_MODEL = 2048
HEAD_DIM = 64
ATTN_WIDTH = 1024
N_Q_HEADS = 16
N_KV_HEADS = 4
GROUP = N_Q_HEADS // N_KV_HEADS
KV_WIDTH = 256
WINDOW = 128
CONV_WIDTH = 512
CONV_KERNEL = 31
CONV_HALO = 32
SGU_WIDTH = 512
SGU_HEADS = 8
CHUNK = 128
D_FF = 4 * D_MODEL
D_IN = ATTN_WIDTH + 2 * KV_WIDTH + 2 * CONV_WIDTH + 2 * SGU_WIDTH
EPS = 1e-6
NEG_INF = -1e30
LANES = 128
HALF = LANES // 2
SUBLANES = 8
SOFTMAX_ROWS = 64
CONV_ROWS = 32
BLOCK = WINDOW
MXU_TILE = 256
_DONE = object()
CONV_TAPS_PER_PIECE = 8
ATTENTION_PIECES = 5 + 2 * N_KV_HEADS
CONV_PIECES = 1 + (BLOCK // CONV_ROWS) * -(-CONV_KERNEL // CONV_TAPS_PER_PIECE)
MLP_PIECES = 16
OUT_PROJ_PIECES = 2 * (D_MODEL // MXU_TILE)
SPLIT_K = ATTN_WIDTH
SPLIT_V = SPLIT_K + KV_WIDTH
SPLIT_CONV = SPLIT_V + KV_WIDTH
SPLIT_SGU = SPLIT_CONV + 2 * CONV_WIDTH

VMEM_LIMIT = 60 * 1024 * 1024


def _rms_scale(x):
    return lax.rsqrt(jnp.mean(x * x, axis=-1, keepdims=True) + EPS)


def _layer_norm(x, g, b):
    mu = jnp.mean(x, axis=-1, keepdims=True)
    xc = x - mu
    return xc * lax.rsqrt(jnp.mean(xc * xc, axis=-1, keepdims=True) + EPS) * g + b


def _cast_kernel(w_ref, o_ref):
    o_ref[...] = w_ref[...].astype(BF16)


def _cast_rows(w, l, block_rows, src_block=lambda r: r, name="cast_rows"):
    _, K, N = w.shape
    return pl.pallas_call(
        _cast_kernel,
        grid=(K // block_rows,),
        in_specs=[pl.BlockSpec((None, block_rows, N), lambda r: (l, src_block(r), 0))],
        out_specs=pl.BlockSpec((block_rows, N), lambda r: (r, 0)),
        out_shape=jax.ShapeDtypeStruct((K, N), BF16),
        compiler_params=pltpu.CompilerParams(
            dimension_semantics=("arbitrary",), vmem_limit_bytes=VMEM_LIMIT),
        name=name,
    )(w)


def _cast_column_chunks(w, l, tf):
    _, K, N = w.shape
    return pl.pallas_call(
        _cast_kernel,
        grid=(N // tf,),
        in_specs=[pl.BlockSpec((None, K, tf), lambda c: (l, 0, c))],
        out_specs=pl.BlockSpec((None, K, tf), lambda c: (c, 0, 0)),
        out_shape=jax.ShapeDtypeStruct((N // tf, K, tf), BF16),
        compiler_params=pltpu.CompilerParams(
            dimension_semantics=("arbitrary",), vmem_limit_bytes=VMEM_LIMIT),
        name="cast_column_chunks",
    )(w)


def _attn_src_head(c):
    permuted = 2 * GROUP * (c // (2 * GROUP)) + GROUP * (c % 2) + (c // 2) % GROUP
    return jnp.where(c < N_Q_HEADS, permuted, c)


def _in_proj_kernel(x_ref, g_ref, w_ref, o_ref):
    x = x_ref[...]
    h = (x * _rms_scale(x) * g_ref[...]).astype(BF16)
    o_ref[...] = jnp.dot(h, w_ref[...], preferred_element_type=F32)


def _in_proj(x2, g, w, tm):
    T = x2.shape[0]
    return pl.pallas_call(
        _in_proj_kernel,
        grid=(T // tm,),
        in_specs=[
            pl.BlockSpec((tm, D_MODEL), lambda i: (i, 0)),
            pl.BlockSpec((1, D_MODEL), lambda i: (0, 0)),
            pl.BlockSpec((D_MODEL, D_IN), lambda i: (0, 0), pipeline_mode=pl.Buffered(1)),
        ],
        out_specs=pl.BlockSpec((tm, D_IN), lambda i: (i, 0)),
        out_shape=jax.ShapeDtypeStruct((T, D_IN), F32),
        compiler_params=pltpu.CompilerParams(
            dimension_semantics=("arbitrary",), vmem_limit_bytes=VMEM_LIMIT),
        name="in_proj",
    )(x2, g, w)


def _head_sumsq(x, ones_bd):
    x2 = x * x
    hi = x2.astype(BF16)
    lo = (x2 - hi.astype(F32)).astype(BF16)
    return (jnp.dot(hi, ones_bd, preferred_element_type=F32)
            + jnp.dot(lo, ones_bd, preferred_element_type=F32))


def _mixers_init(cw_ref, sw_ref, w8_ref, wsg_ref, ones_ref, mask_ref):
    r256 = lax.broadcasted_iota(jnp.int32, (KV_WIDTH, KV_WIDTH), 0)
    c256 = lax.broadcasted_iota(jnp.int32, (KV_WIDTH, KV_WIDTH), 1)
    ones_ref[...] = jnp.where((r256 // HEAD_DIM) == (c256 // HEAD_DIM), 1.0, 0.0).astype(BF16)
    for j in range(CONV_KERNEL):
        w8_ref[j] = jnp.broadcast_to(cw_ref[j:j + 1, :], (SUBLANES, CONV_WIDTH))
    ri = lax.broadcasted_iota(jnp.int32, (2 * CHUNK, CHUNK), 0) % CHUNK
    cj = lax.broadcasted_iota(jnp.int32, (2 * CHUNK, CHUNK), 1)
    for p in range(SGU_HEADS // 2):
        wsg_ref[p] = jnp.where(cj <= ri, sw_ref[p], 0.0).astype(BF16)
    qi = lax.broadcasted_iota(jnp.int32, (WINDOW, 2 * WINDOW), 0)
    sj = lax.broadcasted_iota(jnp.int32, (WINDOW, 2 * WINDOW), 1)
    rel = qi + WINDOW - sj
    mask_ref[1] = jnp.where((rel >= 0) & (rel < WINDOW), 1.0, 0.0)


def _mixers_prelude(t, krep_ref, vwin_ref, tail_ref, mask_ref):
    @pl.when(t == 0)
    def _():
        krep_ref[...] = jnp.zeros_like(krep_ref)
        vwin_ref[...] = jnp.zeros_like(vwin_ref)
        tail_ref[1] = jnp.zeros((CONV_HALO, CONV_WIDTH), F32)

    qi = lax.broadcasted_iota(jnp.int32, (WINDOW, 2 * WINDOW), 0)
    sj = (lax.broadcasted_iota(jnp.int32, (WINDOW, 2 * WINDOW), 1) + WINDOW) % (2 * WINDOW)
    rel = qi + WINDOW - sj
    mask_ref[0] = jnp.where((rel >= 0) & (rel < WINDOW) & (sj >= (1 - jnp.minimum(t, 1)) * WINDOW),
                            1.0, 0.0)


def _exact_zero(v):
    bits = lax.bitcast_convert_type(v, jnp.int32)
    return lax.shift_right_logical(lax.shift_right_logical(bits, 16), 16).astype(F32)


def _low_half():
    return lax.broadcasted_iota(jnp.int32, (WINDOW, LANES), 1) < HALF


def _attention_pieces(parity, q_ref, k_ref, v_ref, gq_ref, gk_ref, sinks_ref, og_ref, o_ref,
                      krep_ref, vwin_ref, ones_ref, mask_ref):
    low_half = _low_half()
    ones_bd = ones_ref[...]
    colblk = lax.broadcasted_iota(jnp.int32, (WINDOW, KV_WIDTH), 1) // HEAD_DIM
    gq = gq_ref[...] * (HEAD_DIM ** -0.5)
    k = k_ref[...]
    kn = k * lax.rsqrt(_head_sumsq(k, ones_bd) * (1.0 / HEAD_DIM) + EPS) * gk_ref[...]
    cur = pl.ds(pl.multiple_of(parity * WINDOW, WINDOW), WINDOW)
    vwin_ref[cur, :] = v_ref[...].astype(BF16)
    for p in range(N_KV_HEADS // 2):
        blk = kn[:, p * LANES:(p + 1) * LANES]
        swapped = pltpu.roll(blk, HALF, 1)
        even = jnp.where(low_half, blk, swapped).astype(BF16)
        odd = jnp.where(low_half, swapped, blk).astype(BF16)
        krep_ref[2 * p, cur, :] = jnp.concatenate([even, even], axis=1)
        krep_ref[2 * p + 1, cur, :] = jnp.concatenate([odd, odd], axis=1)

    yield
    lhs = []
    for kh in range(N_KV_HEADS):
        qs = q_ref[:, kh * KV_WIDTH:(kh + 1) * KV_WIDTH]
        qn = (qs * lax.rsqrt(_head_sumsq(qs, ones_bd) * (1.0 / HEAD_DIM) + EPS)
              * gq[:, kh * KV_WIDTH:(kh + 1) * KV_WIDTH]).astype(BF16)
        lhs.append(jnp.concatenate(
            [jnp.where(colblk == g, qn, jnp.zeros_like(qn)) for g in range(GROUP)], axis=0))
    yield
    all_logits = [lax.dot_general(lhs[kh], krep_ref[kh], (((1,), (1,)), ((), ())),
                                  preferred_element_type=F32)
                  for kh in range(N_KV_HEADS)]
    yield
    all_probs = []
    for kh in range(N_KV_HEADS):
        logits = all_logits[kh]
        probs = []
        for r0 in range(0, GROUP * WINDOW, SOFTMAX_ROWS):
            q0 = r0 % WINDOW
            valid = mask_ref[parity, q0:q0 + SOFTMAX_ROWS, :] > 0.5
            lg = jnp.where(valid, logits[r0:r0 + SOFTMAX_ROWS], NEG_INF)
            sink = sinks_ref[kh * GROUP + r0 // WINDOW]
            m = jnp.maximum(jnp.max(lg, axis=-1, keepdims=True), sink)
            pexp = jnp.exp(lg - m)
            denom = jnp.sum(pexp, axis=-1, keepdims=True) + jnp.exp(sink - m)
            probs.append((pexp * (1.0 / denom)).astype(BF16))
            if (r0 // SOFTMAX_ROWS) % 4 == 3:
                yield
        all_probs.append(jnp.concatenate(probs, axis=0))
    outs = [jnp.dot(all_probs[kh], vwin_ref[...], preferred_element_type=F32)
            for kh in range(N_KV_HEADS)]
    yield
    y = jnp.concatenate(
        [jnp.where(low_half,
                   outs[2 * p][g * WINDOW:(g + 1) * WINDOW, p * LANES:(p + 1) * LANES],
                   outs[2 * p + 1][g * WINDOW:(g + 1) * WINDOW, p * LANES:(p + 1) * LANES])
         for p in range(N_KV_HEADS // 2) for g in range(GROUP)], axis=1)
    o_ref[:, 0:ATTN_WIDTH] = (y * _rms_scale(y) * og_ref[:, 0:ATTN_WIDTH]).astype(BF16)


def _conv_pieces(parity, ca_ref, cg_ref, cb_ref, clg_ref, clb_ref, og_ref, o_ref,
                 hbuf_ref, tail_ref, hs_ref, w8_ref, anchors):
    hbuf_ref[0:CONV_HALO, :] = tail_ref[1 - parity]
    hbuf_ref[CONV_HALO:CONV_HALO + BLOCK, :] = ca_ref[...] * jax.nn.sigmoid(cg_ref[...])
    tail_ref[parity] = hbuf_ref[BLOCK:BLOCK + CONV_HALO, :]
    for r in range(1, SUBLANES):
        hs_ref[r - 1] = hbuf_ref[r:r + BLOCK + CONV_HALO - SUBLANES, :]
    groups = CONV_ROWS // SUBLANES
    for r0 in range(0, BLOCK, CONV_ROWS):
        yield
        accs = [None] * groups
        for j in range(CONV_KERNEL):
            if j % CONV_TAPS_PER_PIECE == 0 and j:
                yield
            a, r = divmod(CONV_HALO - (CONV_KERNEL - 1) + j, SUBLANES)
            w8 = w8_ref[j]
            for g in range(groups):
                base = r0 + SUBLANES * (a + g)
                if r == 0:
                    src = hbuf_ref[base:base + SUBLANES, :]
                else:
                    src = hs_ref[r - 1, base:base + SUBLANES, :]
                accs[g] = src * w8 if accs[g] is None else accs[g] + src * w8
        acc = jnp.concatenate(accs, axis=0)
        hc = _layer_norm(acc + cb_ref[...], clg_ref[...], clb_ref[...])
        yc = hc * jax.nn.sigmoid(hc)
        o_ref[r0:r0 + CONV_ROWS, ATTN_WIDTH:ATTN_WIDTH + CONV_WIDTH] = (
            yc * _rms_scale(yc) * og_ref[:, ATTN_WIDTH:ATTN_WIDTH + CONV_WIDTH]).astype(BF16)
        anchors.append(yc[0:2 * SUBLANES, 0:LANES])


def _sgu_pieces(su_ref, sv_ref, slg_ref, slb_ref, sb_ref, og_ref, o_ref, wsg_ref):
    low_half = _low_half()
    vn = _layer_norm(sv_ref[...], slg_ref[...], slb_ref[...]).astype(BF16)
    ys = []
    for p in range(SGU_HEADS // 2):
        r = jnp.dot(wsg_ref[p], vn[:, p * LANES:(p + 1) * LANES],
                    preferred_element_type=F32)
        sp = jnp.where(low_half, r[0:CHUNK], r[CHUNK:2 * CHUNK])
        sp = sp + sb_ref[:, p * LANES:(p + 1) * LANES]
        ys.append(su_ref[:, p * LANES:(p + 1) * LANES] * sp)
    ysg = jnp.concatenate(ys, axis=1)
    o_ref[:, ATTN_WIDTH + CONV_WIDTH:] = (
        ysg * _rms_scale(ysg) * og_ref[:, ATTN_WIDTH + CONV_WIDTH:]).astype(BF16)
    yield


def _layer_tail_kernel(proj_ref, x_ref,
                       gq_ref, gk_ref, sinks_ref, cw_ref, cb_ref, clg_ref, clb_ref,
                       slg_ref, slb_ref, sw_ref, sb_ref, og_ref,
                       wo_ref, g2_ref, wu_ref, wd_ref,
                       o_ref,
                       mix_ref, x1_ref, h_ref, uu_ref,
                       krep_ref, vwin_ref, hbuf_ref, tail_ref, hs_ref, w8_ref, wsg_ref, ones_ref,
                       mask_ref, *, n_tiles, blocks_per_tile, blocks_per_seq):
    i = pl.program_id(0)
    f = pl.program_id(1)
    n_f = pl.num_programs(1)
    first = i == 0
    last = i == n_tiles
    even = f % 2 == 0
    blk = jnp.minimum(i, n_tiles - 1) * blocks_per_tile + f // 2
    rows = pl.ds(pl.multiple_of((f // 2) * BLOCK, BLOCK), BLOCK)
    h_mlp = (i + 1) % 2
    h_new = i % 2

    @pl.when(first & (f == 0))
    def _():
        _mixers_init(cw_ref, sw_ref, w8_ref, wsg_ref, ones_ref, mask_ref)

    @pl.when((f == 0) & jnp.logical_not(first))
    def _():
        o_ref[...] = x1_ref[...]

    def mlp_chunk(anchors=()):
        tf = wu_ref.shape[1]
        kh = D_MODEL // 2
        for n in range(0, tf, MXU_TILE):
            u = jnp.dot(h_ref[h_mlp, :, 0:kh], wu_ref[0:kh, n:n + MXU_TILE],
                        preferred_element_type=F32)
            yield
            u = u + jnp.dot(h_ref[h_mlp, :, kh:], wu_ref[kh:, n:n + MXU_TILE],
                            preferred_element_type=F32)
            u = jnp.maximum(u, 0.0)
            uu_ref[:, n:n + MXU_TILE] = (u * u).astype(BF16)
            yield
        for m in range(0, D_MODEL, MXU_TILE):
            while anchors:
                uu_ref[0:2 * SUBLANES, 0:LANES] += _exact_zero(anchors.pop()).astype(BF16)
            o_ref[:, m:m + MXU_TILE] += jnp.dot(uu_ref[...], wd_ref[:, m:m + MXU_TILE],
                                                preferred_element_type=F32)
            yield

    t_seq = blk % blocks_per_seq
    parity = t_seq % 2
    mix = mix_ref.at[rows]

    def col(a, w):
        return proj_ref.at[:, a:a + w]

    def attention_streams():
        _mixers_prelude(t_seq, krep_ref, vwin_ref, tail_ref, mask_ref)
        x1_ref[rows, :] = x_ref[...]
        return [(_attention_pieces(parity, col(0, ATTN_WIDTH), col(SPLIT_K, KV_WIDTH),
                                   col(SPLIT_V, KV_WIDTH), gq_ref, gk_ref, sinks_ref, og_ref, mix,
                                   krep_ref, vwin_ref, ones_ref, mask_ref), ATTENTION_PIECES)]

    def conv_sgu_streams(anchors):
        return [
            (_conv_pieces(parity, col(SPLIT_CONV, CONV_WIDTH),
                          col(SPLIT_CONV + CONV_WIDTH, CONV_WIDTH), cb_ref, clg_ref, clb_ref,
                          og_ref, mix, hbuf_ref, tail_ref, hs_ref, w8_ref, anchors), CONV_PIECES),
            (_sgu_pieces(col(SPLIT_SGU, SGU_WIDTH), col(SPLIT_SGU + SGU_WIDTH, SGU_WIDTH),
                         slg_ref, slb_ref, sb_ref, og_ref, mix, wsg_ref), 1)]

    def out_proj_pieces():
        ssq = None
        for n in range(0, D_MODEL, MXU_TILE):
            x1 = x1_ref[:, n:n + MXU_TILE] + jnp.dot(mix_ref[...], wo_ref[:, n:n + MXU_TILE],
                                                     preferred_element_type=F32)
            x1_ref[:, n:n + MXU_TILE] = x1
            part = jnp.sum(x1 * x1, axis=-1, keepdims=True)
            ssq = part if ssq is None else ssq + part
            yield
        scale = lax.rsqrt(ssq * (1.0 / D_MODEL) + EPS)
        for n in range(0, D_MODEL, MXU_TILE):
            h_ref[h_new, :, n:n + MXU_TILE] = (
                x1_ref[:, n:n + MXU_TILE] * scale * g2_ref[:, n:n + MXU_TILE]).astype(BF16)
            yield

    def interleave(streams):
        done = [0] * len(streams)
        live = set(range(len(streams)))
        while live:
            s = min(live, key=lambda j: (done[j] + 1) / streams[j][1])
            done[s] += 1
            if next(streams[s][0], _DONE) is _DONE:
                live.remove(s)

    middle = jnp.logical_not(first | last)
    is_proj = f == n_f - 1
    odd = jnp.logical_not(even | is_proj)

    @pl.when(even & middle)
    def _():
        interleave(attention_streams() + [(mlp_chunk(), MLP_PIECES)])

    @pl.when(even & first)
    def _():
        interleave(attention_streams())

    @pl.when(odd & middle)
    def _():
        anchors = []
        interleave(conv_sgu_streams(anchors) + [(mlp_chunk(anchors), MLP_PIECES)])

    @pl.when(odd & first)
    def _():
        interleave(conv_sgu_streams([]))

    @pl.when(is_proj & middle)
    def _():
        anchors = []
        interleave(conv_sgu_streams(anchors) + [(mlp_chunk(anchors), MLP_PIECES)])
        interleave([(out_proj_pieces(), OUT_PROJ_PIECES)])

    @pl.when(is_proj & first)
    def _():
        interleave(conv_sgu_streams([]))
        interleave([(out_proj_pieces(), OUT_PROJ_PIECES)])

    @pl.when(last)
    def _():
        interleave([(mlp_chunk(), MLP_PIECES)])


def _layer_tail(proj, x2, B, S, tm, tf, gq, gk, sinks, cw, cb, clg, clb, slg, slb, sw, sb, og,
                wo, g2, wu, wd):
    T = B * S
    n_tiles = T // tm
    bpt = tm // BLOCK
    n_f = D_FF // tf
    assert n_f == 2 * bpt, "each mixer block takes two MLP chunk steps"
    assert S % BLOCK == 0 and T % tm == 0

    def blk(i, f):
        return jnp.minimum(i, n_tiles - 1) * bpt + f // 2

    def whole(shape, **kw):
        return pl.BlockSpec(shape, lambda i, f: (0,) * len(shape), **kw)

    return pl.pallas_call(
        functools.partial(_layer_tail_kernel, n_tiles=n_tiles, blocks_per_tile=bpt,
                          blocks_per_seq=S // BLOCK),
        grid=(n_tiles + 1, n_f),
        in_specs=[
            pl.BlockSpec((BLOCK, D_IN), lambda i, f: (blk(i, f), 0)),
            pl.BlockSpec((BLOCK, D_MODEL), lambda i, f: (blk(i, f), 0)),
            whole((1, ATTN_WIDTH)), whole((1, KV_WIDTH)),
            pl.BlockSpec(memory_space=pltpu.MemorySpace.SMEM),
            whole((CONV_KERNEL, CONV_WIDTH)), whole((1, CONV_WIDTH)),
            whole((1, CONV_WIDTH)), whole((1, CONV_WIDTH)),
            whole((1, SGU_WIDTH)), whole((1, SGU_WIDTH)),
            whole((SGU_HEADS // 2, 2 * CHUNK, CHUNK)), whole((CHUNK, SGU_WIDTH)),
            whole((1, D_MODEL)),
            whole((D_MODEL, D_MODEL), pipeline_mode=pl.Buffered(1)),
            whole((1, D_MODEL)),
            pl.BlockSpec((None, D_MODEL, tf), lambda i, f: (f, 0, 0)),
            pl.BlockSpec((tf, D_MODEL), lambda i, f: (f, 0)),
        ],
        out_specs=pl.BlockSpec((tm, D_MODEL), lambda i, f: (jnp.maximum(i - 1, 0), 0)),
        out_shape=jax.ShapeDtypeStruct((T, D_MODEL), F32),
        scratch_shapes=[
            pltpu.VMEM((tm, D_MODEL), BF16),
            pltpu.VMEM((tm, D_MODEL), F32),
            pltpu.VMEM((2, tm, D_MODEL), BF16),
            pltpu.VMEM((tm, tf), BF16),
            pltpu.VMEM((N_KV_HEADS, 2 * WINDOW, KV_WIDTH), BF16),
            pltpu.VMEM((2 * WINDOW, KV_WIDTH), BF16),
            pltpu.VMEM((CONV_HALO + BLOCK, CONV_WIDTH), F32),
            pltpu.VMEM((2, CONV_HALO, CONV_WIDTH), F32),
            pltpu.VMEM((SUBLANES - 1, CONV_HALO + BLOCK - SUBLANES, CONV_WIDTH), F32),
            pltpu.VMEM((CONV_KERNEL, SUBLANES, CONV_WIDTH), F32),
            pltpu.VMEM((SGU_HEADS // 2, 2 * CHUNK, CHUNK), BF16),
            pltpu.VMEM((KV_WIDTH, KV_WIDTH), BF16),
            pltpu.VMEM((2, WINDOW, 2 * WINDOW), F32),
        ],
        compiler_params=pltpu.CompilerParams(
            dimension_semantics=("arbitrary", "arbitrary"), vmem_limit_bytes=VMEM_LIMIT),
        name="layer_tail",
    )(proj, x2, gq, gk, sinks, cw, cb, clg, clb, slg, slb, sw, sb, og, wo, g2, wu, wd)


def _attn_perm():
    perm = np.empty((ATTN_WIDTH,), np.int32)
    for p in range(N_KV_HEADS // 2):
        for g in range(GROUP):
            for e in range(2):
                head = GROUP * (2 * p + e) + g
                dst = (GROUP * p + g) * LANES + e * HALF
                perm[dst:dst + HEAD_DIM] = np.arange(head * HEAD_DIM, (head + 1) * HEAD_DIM)
    return perm


def _pick_tile(n, pref):
    while n % pref:
        pref //= 2
    return pref


def kernel(x, ln1_g, w_in, q_norm_g, k_norm_g, sinks, conv_w, conv_b, conv_ln_g, conv_ln_b,
           sgu_ln_g, sgu_ln_b, sgu_w, sgu_b, out_norm_g, w_out, ln2_g, w_up, w_down):
    B, S, D = x.shape
    depth = w_in.shape[0]
    assert D == D_MODEL and S % WINDOW == 0
    T = B * S
    tm_proj = _pick_tile(T, 512)
    tm_tail = _pick_tile(T, 512)
    tf = D_FF // (2 * (tm_tail // BLOCK))
    perm = np.concatenate([_attn_perm(), np.arange(ATTN_WIDTH, D_MODEL, dtype=np.int32)])

    x2 = x.reshape(T, D)
    for l in range(depth):
        proj = _in_proj(x2, ln1_g[l][None, :], _cast_rows(w_in, l, 256, name="cast_w_in"), tm_proj)
        x2 = _layer_tail(
            proj, x2, B, S, tm_tail, tf,
            jnp.tile(q_norm_g[l], N_Q_HEADS)[None, :], jnp.tile(k_norm_g[l], N_KV_HEADS)[None, :],
            sinks[l], conv_w[l], conv_b[l][None, :], conv_ln_g[l][None, :], conv_ln_b[l][None, :],
            sgu_ln_g[l][None, :], sgu_ln_b[l][None, :],
            sgu_w[l].reshape(SGU_HEADS // 2, 2 * CHUNK, CHUNK),
            jnp.repeat(sgu_b[l].T, HEAD_DIM, axis=1),
            out_norm_g[l][perm][None, :],
            _cast_rows(w_out, l, HEAD_DIM, _attn_src_head, name="cast_w_out"), ln2_g[l][None, :],
            _cast_column_chunks(w_up, l, tf), _cast_rows(w_down, l, 512, name="cast_w_down"))
    return x2.reshape(B, S, D)
```

```python
import functools

import jax
import jax.numpy as jnp
import numpy as np
from jax import lax
from jax.experimental import pallas as pl
from jax.experimental.pallas import tpu as pltpu

F32 = jnp.float32
BF16 = jnp.bfloat16

D_MODEL = 2048
HEAD_DIM = 64
ATTN_WIDTH = 1024
N_Q_HEADS = 16
N_KV_HEADS = 4
GROUP = N_Q_HEADS // N_KV_HEADS
KV_WIDTH = 256
WINDOW = 128
CONV_WIDTH = 512
CONV_KERNEL = 31
CONV_HALO = 32
SGU_WIDTH = 512
SGU_HEADS = 8
CHUNK = 128
D_FF = 4 * D_MODEL
D_IN = ATTN_WIDTH + 2 * KV_WIDTH + 2 * CONV_WIDTH + 2 * SGU_WIDTH
EPS = 1e-6
NEG_INF = -1e30
LANES = 128
HALF = LANES // 2
SUBLANES = 8
SOFTMAX_ROWS = 64
CONV_ROWS = 32
BLOCK = WINDOW
MXU_TILE = 256
_DONE = object()
CONV_TAPS_PER_PIECE = 8
ATTENTION_PIECES = 5 + 2 * N_KV_HEADS
CONV_PIECES = 1 + (BLOCK // CONV_ROWS) * -(-CONV_KERNEL // CONV_TAPS_PER_PIECE)
MLP_PIECES = 16
OUT_PROJ_PIECES = 2 * (D_MODEL // MXU_TILE)
SPLIT_K = ATTN_WIDTH
SPLIT_V = SPLIT_K + KV_WIDTH
SPLIT_CONV = SPLIT_V + KV_WIDTH
SPLIT_SGU = SPLIT_CONV + 2 * CONV_WIDTH

VMEM_LIMIT = 60 * 1024 * 1024


def _rms_scale(x):
    return lax.rsqrt(jnp.mean(x * x, axis=-1, keepdims=True) + EPS)


def _layer_norm(x, g, b):
    mu = jnp.mean(x, axis=-1, keepdims=True)
    xc = x - mu
    return xc * lax.rsqrt(jnp.mean(xc * xc, axis=-1, keepdims=True) + EPS) * g + b


def _cast_kernel(w_ref, o_ref):
    o_ref[...] = w_ref[...].astype(BF16)


def _cast_rows(w, l, block_rows, src_block=lambda r: r, name="cast_rows"):
    _, K, N = w.shape
    return pl.pallas_call(
        _cast_kernel,
        grid=(K // block_rows,),
        in_specs=[pl.BlockSpec((None, block_rows, N), lambda r: (l, src_block(r), 0))],
        out_specs=pl.BlockSpec((block_rows, N), lambda r: (r, 0)),
        out_shape=jax.ShapeDtypeStruct((K, N), BF16),
        compiler_params=pltpu.CompilerParams(
            dimension_semantics=("arbitrary",), vmem_limit_bytes=VMEM_LIMIT),
        name=name,
    )(w)


def _cast_column_chunks(w, l, tf):
    _, K, N = w.shape
    return pl.pallas_call(
        _cast_kernel,
        grid=(N // tf,),
        in_specs=[pl.BlockSpec((None, K, tf), lambda c: (l, 0, c))],
        out_specs=pl.BlockSpec((None, K, tf), lambda c: (c, 0, 0)),
        out_shape=jax.ShapeDtypeStruct((N // tf, K, tf), BF16),
        compiler_params=pltpu.CompilerParams(
            dimension_semantics=("arbitrary",), vmem_limit_bytes=VMEM_LIMIT),
        name="cast_column_chunks",
    )(w)


def _attn_src_head(c):
    permuted = 2 * GROUP * (c // (2 * GROUP)) + GROUP * (c % 2) + (c // 2) % GROUP
    return jnp.where(c < N_Q_HEADS, permuted, c)


def _in_proj_kernel(x_ref, g_ref, w_ref, o_ref):
    x = x_ref[...]
    h = (x * _rms_scale(x) * g_ref[...]).astype(BF16)
    o_ref[...] = jnp.dot(h, w_ref[...], preferred_element_type=F32)


def _in_proj(x2, g, w, tm):
    T = x2.shape[0]
    return pl.pallas_call(
        _in_proj_kernel,
        grid=(T // tm,),
        in_specs=[
            pl.BlockSpec((tm, D_MODEL), lambda i: (i, 0)),
            pl.BlockSpec((1, D_MODEL), lambda i: (0, 0)),
            pl.BlockSpec((D_MODEL, D_IN), lambda i: (0, 0), pipeline_mode=pl.Buffered(1)),
        ],
        out_specs=pl.BlockSpec((tm, D_IN), lambda i: (i, 0)),
        out_shape=jax.ShapeDtypeStruct((T, D_IN), F32),
        compiler_params=pltpu.CompilerParams(
            dimension_semantics=("arbitrary",), vmem_limit_bytes=VMEM_LIMIT),
        name="in_proj",
    )(x2, g, w)


def _head_sumsq(x, ones_bd):
    x2 = x * x
    hi = x2.astype(BF16)
    lo = (x2 - hi.astype(F32)).astype(BF16)
    return (jnp.dot(hi, ones_bd, preferred_element_type=F32)
            + jnp.dot(lo, ones_bd, preferred_element_type=F32))


def _mixers_init(cw_ref, sw_ref, w8_ref, wsg_ref, ones_ref, mask_ref):
    r256 = lax.broadcasted_iota(jnp.int32, (KV_WIDTH, KV_WIDTH), 0)
    c256 = lax.broadcasted_iota(jnp.int32, (KV_WIDTH, KV_WIDTH), 1)
    ones_ref[...] = jnp.where((r256 // HEAD_DIM) == (c256 // HEAD_DIM), 1.0, 0.0).astype(BF16)
    for j in range(CONV_KERNEL):
        w8_ref[j] = jnp.broadcast_to(cw_ref[j:j + 1, :], (SUBLANES, CONV_WIDTH))
    ri = lax.broadcasted_iota(jnp.int32, (2 * CHUNK, CHUNK), 0) % CHUNK
    cj = lax.broadcasted_iota(jnp.int32, (2 * CHUNK, CHUNK), 1)
    for p in range(SGU_HEADS // 2):
        wsg_ref[p] = jnp.where(cj <= ri, sw_ref[p], 0.0).astype(BF16)
    qi = lax.broadcasted_iota(jnp.int32, (WINDOW, 2 * WINDOW), 0)
    sj = lax.broadcasted_iota(jnp.int32, (WINDOW, 2 * WINDOW), 1)
    rel = qi + WINDOW - sj
    mask_ref[1] = jnp.where((rel >= 0) & (rel < WINDOW), 1.0, 0.0)


def _mixers_prelude(t, krep_ref, vwin_ref, tail_ref, mask_ref):
    @pl.when(t == 0)
    def _():
        krep_ref[...] = jnp.zeros_like(krep_ref)
        vwin_ref[...] = jnp.zeros_like(vwin_ref)
        tail_ref[1] = jnp.zeros((CONV_HALO, CONV_WIDTH), F32)

    qi = lax.broadcasted_iota(jnp.int32, (WINDOW, 2 * WINDOW), 0)
    sj = (lax.broadcasted_iota(jnp.int32, (WINDOW, 2 * WINDOW), 1) + WINDOW) % (2 * WINDOW)
    rel = qi + WINDOW - sj
    mask_ref[0] = jnp.where((rel >= 0) & (rel < WINDOW) & (sj >= (1 - jnp.minimum(t, 1)) * WINDOW),
                            1.0, 0.0)


def _exact_zero(v):
    bits = lax.bitcast_convert_type(v, jnp.int32)
    return lax.shift_right_logical(lax.shift_right_logical(bits, 16), 16).astype(F32)


def _low_half():
    return lax.broadcasted_iota(jnp.int32, (WINDOW, LANES), 1) < HALF


def _attention_pieces(parity, q_ref, k_ref, v_ref, gq_ref, gk_ref, sinks_ref, og_ref, o_ref,
                      krep_ref, vwin_ref, ones_ref, mask_ref):
    low_half = _low_half()
    ones_bd = ones_ref[...]
    colblk = lax.broadcasted_iota(jnp.int32, (WINDOW, KV_WIDTH), 1) // HEAD_DIM
    gq = gq_ref[...] * (HEAD_DIM ** -0.5)
    k = k_ref[...]
    kn = k * lax.rsqrt(_head_sumsq(k, ones_bd) * (1.0 / HEAD_DIM) + EPS) * gk_ref[...]
    cur = pl.ds(pl.multiple_of(parity * WINDOW, WINDOW), WINDOW)
    vwin_ref[cur, :] = v_ref[...].astype(BF16)
    for p in range(N_KV_HEADS // 2):
        blk = kn[:, p * LANES:(p + 1) * LANES]
        swapped = pltpu.roll(blk, HALF, 1)
        even = jnp.where(low_half, blk, swapped).astype(BF16)
        odd = jnp.where(low_half, swapped, blk).astype(BF16)
        krep_ref[2 * p, cur, :] = jnp.concatenate([even, even], axis=1)
        krep_ref[2 * p + 1, cur, :] = jnp.concatenate([odd, odd], axis=1)

    yield
    lhs = []
    for kh in range(N_KV_HEADS):
        qs = q_ref[:, kh * KV_WIDTH:(kh + 1) * KV_WIDTH]
        qn = (qs * lax.rsqrt(_head_sumsq(qs, ones_bd) * (1.0 / HEAD_DIM) + EPS)
              * gq[:, kh * KV_WIDTH:(kh + 1) * KV_WIDTH]).astype(BF16)
        lhs.append(jnp.concatenate(
            [jnp.where(colblk == g, qn, jnp.zeros_like(qn)) for g in range(GROUP)], axis=0))
    yield
    all_logits = [lax.dot_general(lhs[kh], krep_ref[kh], (((1,), (1,)), ((), ())),
                                  preferred_element_type=F32)
                  for kh in range(N_KV_HEADS)]
    yield
    all_probs = []
    for kh in range(N_KV_HEADS):
        logits = all_logits[kh]
        probs = []
        for r0 in range(0, GROUP * WINDOW, SOFTMAX_ROWS):
            q0 = r0 % WINDOW
            valid = mask_ref[parity, q0:q0 + SOFTMAX_ROWS, :] > 0.5
            lg = jnp.where(valid, logits[r0:r0 + SOFTMAX_ROWS], NEG_INF)
            sink = sinks_ref[kh * GROUP + r0 // WINDOW]
            m = jnp.maximum(jnp.max(lg, axis=-1, keepdims=True), sink)
            pexp = jnp.exp(lg - m)
            denom = jnp.sum(pexp, axis=-1, keepdims=True) + jnp.exp(sink - m)
            probs.append((pexp * (1.0 / denom)).astype(BF16))
            if (r0 // SOFTMAX_ROWS) % 4 == 3:
                yield
        all_probs.append(jnp.concatenate(probs, axis=0))
    outs = [jnp.dot(all_probs[kh], vwin_ref[...], preferred_element_type=F32)
            for kh in range(N_KV_HEADS)]
    yield
    y = jnp.concatenate(
        [jnp.where(low_half,
                   outs[2 * p][g * WINDOW:(g + 1) * WINDOW, p * LANES:(p + 1) * LANES],
                   outs[2 * p + 1][g * WINDOW:(g + 1) * WINDOW, p * LANES:(p + 1) * LANES])
         for p in range(N_KV_HEADS // 2) for g in range(GROUP)], axis=1)
    o_ref[:, 0:ATTN_WIDTH] = (y * _rms_scale(y) * og_ref[:, 0:ATTN_WIDTH]).astype(BF16)


def _conv_pieces(parity, ca_ref, cg_ref, cb_ref, clg_ref, clb_ref, og_ref, o_ref,
                 hbuf_ref, tail_ref, hs_ref, w8_ref, anchors):
    hbuf_ref[0:CONV_HALO, :] = tail_ref[1 - parity]
    hbuf_ref[CONV_HALO:CONV_HALO + BLOCK, :] = ca_ref[...] * jax.nn.sigmoid(cg_ref[...])
    tail_ref[parity] = hbuf_ref[BLOCK:BLOCK + CONV_HALO, :]
    for r in range(1, SUBLANES):
        hs_ref[r - 1] = hbuf_ref[r:r + BLOCK + CONV_HALO - SUBLANES, :]
    groups = CONV_ROWS // SUBLANES
    for r0 in range(0, BLOCK, CONV_ROWS):
        yield
        accs = [None] * groups
        for j in range(CONV_KERNEL):
            if j % CONV_TAPS_PER_PIECE == 0 and j:
                yield
            a, r = divmod(CONV_HALO - (CONV_KERNEL - 1) + j, SUBLANES)
            w8 = w8_ref[j]
            for g in range(groups):
                base = r0 + SUBLANES * (a + g)
                if r == 0:
                    src = hbuf_ref[base:base + SUBLANES, :]
                else:
                    src = hs_ref[r - 1, base:base + SUBLANES, :]
                accs[g] = src * w8 if accs[g] is None else accs[g] + src * w8
        acc = jnp.concatenate(accs, axis=0)
        hc = _layer_norm(acc + cb_ref[...], clg_ref[...], clb_ref[...])
        yc = hc * jax.nn.sigmoid(hc)
        o_ref[r0:r0 + CONV_ROWS, ATTN_WIDTH:ATTN_WIDTH + CONV_WIDTH] = (
            yc * _rms_scale(yc) * og_ref[:, ATTN_WIDTH:ATTN_WIDTH + CONV_WIDTH]).astype(BF16)
        anchors.append(yc[0:2 * SUBLANES, 0:LANES])


def _sgu_pieces(su_ref, sv_ref, slg_ref, slb_ref, sb_ref, og_ref, o_ref, wsg_ref):
    low_half = _low_half()
    vn = _layer_norm(sv_ref[...], slg_ref[...], slb_ref[...]).astype(BF16)
    ys = []
    for p in range(SGU_HEADS // 2):
        r = jnp.dot(wsg_ref[p], vn[:, p * LANES:(p + 1) * LANES],
                    preferred_element_type=F32)
        sp = jnp.where(low_half, r[0:CHUNK], r[CHUNK:2 * CHUNK])
        sp = sp + sb_ref[:, p * LANES:(p + 1) * LANES]
        ys.append(su_ref[:, p * LANES:(p + 1) * LANES] * sp)
    ysg = jnp.concatenate(ys, axis=1)
    o_ref[:, ATTN_WIDTH + CONV_WIDTH:] = (
        ysg * _rms_scale(ysg) * og_ref[:, ATTN_WIDTH + CONV_WIDTH:]).astype(BF16)
    yield


def _layer_tail_kernel(proj_ref, x_ref,
                       gq_ref, gk_ref, sinks_ref, cw_ref, cb_ref, clg_ref, clb_ref,
                       slg_ref, slb_ref, sw_ref, sb_ref, og_ref,
                       wo_ref, g2_ref, wu_hbm, wd_hbm,
                       o_ref,
                       wu_buf, wd_buf, w_sem, mix_ref, x1_ref, h_ref, uu_ref,
                       krep_ref, vwin_ref, hbuf_ref, tail_ref, hs_ref, w8_ref, wsg_ref, ones_ref,
                       mask_ref, *, n_tiles, blocks_per_tile, blocks_per_seq):
    i = pl.program_id(0)
    g = pl.program_id(1)
    n_g = pl.num_programs(1)
    n_chunks = 2 * n_g
    tf = wu_buf.shape[2]
    first = i == 0
    last = i == n_tiles
    final = last & (g == n_g - 1)
    blk = jnp.minimum(i, n_tiles - 1) * blocks_per_tile + g
    rows = pl.ds(pl.multiple_of(g * BLOCK, BLOCK), BLOCK)
    h_mlp = (i + 1) % 2
    h_new = i % 2

    def weight_copies(chunk, slot):
        return (pltpu.make_async_copy(wu_hbm.at[chunk], wu_buf.at[slot], w_sem.at[0, slot]),
                pltpu.make_async_copy(wd_hbm.at[pl.ds(pl.multiple_of(chunk * tf, tf), tf)],
                                      wd_buf.at[slot], w_sem.at[1, slot]))

    def start_weights(chunk, slot):
        for copy in weight_copies(chunk, slot):
            copy.start()

    def wait_weights(slot):
        for copy in weight_copies(0, slot):
            copy.wait()

    @pl.when(first & (g == 0))
    def _():
        _mixers_init(cw_ref, sw_ref, w8_ref, wsg_ref, ones_ref, mask_ref)
        start_weights(0, 0)
        start_weights(1, 1)

    @pl.when((g == 0) & jnp.logical_not(first))
    def _():
        o_ref[...] = x1_ref[...]

    def mlp_chunk(slot, anchors=()):
        wu_ref, wd_ref = wu_buf.at[slot], wd_buf.at[slot]
        kh = D_MODEL // 2
        for n in range(0, tf, MXU_TILE):
            u = jnp.dot(h_ref[h_mlp, :, 0:kh], wu_ref[0:kh, n:n + MXU_TILE],
                        preferred_element_type=F32)
            yield
            u = u + jnp.dot(h_ref[h_mlp, :, kh:], wu_ref[kh:, n:n + MXU_TILE],
                            preferred_element_type=F32)
            u = jnp.maximum(u, 0.0)
            uu_ref[:, n:n + MXU_TILE] = (u * u).astype(BF16)
            yield
        for m in range(0, D_MODEL, MXU_TILE):
            while anchors:
                uu_ref[0:2 * SUBLANES, 0:LANES] += _exact_zero(anchors.pop()).astype(BF16)
            o_ref[:, m:m + MXU_TILE] += jnp.dot(uu_ref[...], wd_ref[:, m:m + MXU_TILE],
                                                preferred_element_type=F32)
            yield

    t_seq = blk % blocks_per_seq
    parity = t_seq % 2
    mix = mix_ref.at[rows]

    def col(a, w):
        return proj_ref.at[:, a:a + w]

    def attention_streams():
        _mixers_prelude(t_seq, krep_ref, vwin_ref, tail_ref, mask_ref)
        x1_ref[rows, :] = x_ref[...]
        return [(_attention_pieces(parity, col(0, ATTN_WIDTH), col(SPLIT_K, KV_WIDTH),
                                   col(SPLIT_V, KV_WIDTH), gq_ref, gk_ref, sinks_ref, og_ref, mix,
                                   krep_ref, vwin_ref, ones_ref, mask_ref), ATTENTION_PIECES)]

    def conv_sgu_streams(anchors):
        return [
            (_conv_pieces(parity, col(SPLIT_CONV, CONV_WIDTH),
                          col(SPLIT_CONV + CONV_WIDTH, CONV_WIDTH), cb_ref, clg_ref, clb_ref,
                          og_ref, mix, hbuf_ref, tail_ref, hs_ref, w8_ref, anchors), CONV_PIECES),
            (_sgu_pieces(col(SPLIT_SGU, SGU_WIDTH), col(SPLIT_SGU + SGU_WIDTH, SGU_WIDTH),
                         slg_ref, slb_ref, sb_ref, og_ref, mix, wsg_ref), 1)]

    def out_proj_pieces():
        ssq = None
        for n in range(0, D_MODEL, MXU_TILE):
            x1 = x1_ref[:, n:n + MXU_TILE] + jnp.dot(mix_ref[...], wo_ref[:, n:n + MXU_TILE],
                                                     preferred_element_type=F32)
            x1_ref[:, n:n + MXU_TILE] = x1
            part = jnp.sum(x1 * x1, axis=-1, keepdims=True)
            ssq = part if ssq is None else ssq + part
            yield
        scale = lax.rsqrt(ssq * (1.0 / D_MODEL) + EPS)
        for n in range(0, D_MODEL, MXU_TILE):
            h_ref[h_new, :, n:n + MXU_TILE] = (
                x1_ref[:, n:n + MXU_TILE] * scale * g2_ref[:, n:n + MXU_TILE]).astype(BF16)
            yield

    def interleave(streams):
        done = [0] * len(streams)
        live = set(range(len(streams)))
        while live:
            s = min(live, key=lambda j: (done[j] + 1) / streams[j][1])
            done[s] += 1
            if next(streams[s][0], _DONE) is _DONE:
                live.remove(s)

    middle = jnp.logical_not(first | last)
    is_proj = g == n_g - 1

    wait_weights(0)

    @pl.when(middle)
    def _():
        interleave(attention_streams() + [(mlp_chunk(0), MLP_PIECES)])

    @pl.when(first)
    def _():
        interleave(attention_streams())

    @pl.when(last)
    def _():
        interleave([(mlp_chunk(0), MLP_PIECES)])

    @pl.when(jnp.logical_not(final))
    def _():
        start_weights((2 * g + 2) % n_chunks, 0)

    wait_weights(1)

    @pl.when(middle)
    def _():
        anchors = []
        interleave(conv_sgu_streams(anchors) + [(mlp_chunk(1, anchors), MLP_PIECES)])

    @pl.when(first)
    def _():
        interleave(conv_sgu_streams([]))

    @pl.when(last)
    def _():
        interleave([(mlp_chunk(1), MLP_PIECES)])

    @pl.when(is_proj & jnp.logical_not(last))
    def _():
        interleave([(out_proj_pieces(), OUT_PROJ_PIECES)])

    @pl.when(jnp.logical_not(final))
    def _():
        start_weights((2 * g + 3) % n_chunks, 1)


def _layer_tail(proj, x2, B, S, tm, tf, gq, gk, sinks, cw, cb, clg, clb, slg, slb, sw, sb, og,
                wo, g2, wu, wd):
    T = B * S
    n_tiles = T // tm
    bpt = tm // BLOCK
    assert D_FF // tf == 2 * bpt, "each mixer block step runs two MLP chunks"
    assert S % BLOCK == 0 and T % tm == 0

    def blk(i, g):
        return jnp.minimum(i, n_tiles - 1) * bpt + g

    def whole(shape, **kw):
        return pl.BlockSpec(shape, lambda i, g: (0,) * len(shape), **kw)

    return pl.pallas_call(
        functools.partial(_layer_tail_kernel, n_tiles=n_tiles, blocks_per_tile=bpt,
                          blocks_per_seq=S // BLOCK),
        grid=(n_tiles + 1, bpt),
        in_specs=[
            pl.BlockSpec((BLOCK, D_IN), lambda i, g: (blk(i, g), 0)),
            pl.BlockSpec((BLOCK, D_MODEL), lambda i, g: (blk(i, g), 0)),
            whole((1, ATTN_WIDTH)), whole((1, KV_WIDTH)),
            pl.BlockSpec(memory_space=pltpu.MemorySpace.SMEM),
            whole((CONV_KERNEL, CONV_WIDTH)), whole((1, CONV_WIDTH)),
            whole((1, CONV_WIDTH)), whole((1, CONV_WIDTH)),
            whole((1, SGU_WIDTH)), whole((1, SGU_WIDTH)),
            whole((SGU_HEADS // 2, 2 * CHUNK, CHUNK)), whole((CHUNK, SGU_WIDTH)),
            whole((1, D_MODEL)),
            whole((D_MODEL, D_MODEL), pipeline_mode=pl.Buffered(1)),
            whole((1, D_MODEL)),
            pl.BlockSpec(memory_space=pl.ANY),
            pl.BlockSpec(memory_space=pl.ANY),
        ],
        out_specs=pl.BlockSpec((tm, D_MODEL), lambda i, g: (jnp.maximum(i - 1, 0), 0)),
        out_shape=jax.ShapeDtypeStruct((T, D_MODEL), F32),
        scratch_shapes=[
            pltpu.VMEM((2, D_MODEL, tf), BF16),
            pltpu.VMEM((2, tf, D_MODEL), BF16),
            pltpu.SemaphoreType.DMA((2, 2)),
            pltpu.VMEM((tm, D_MODEL), BF16),
            pltpu.VMEM((tm, D_MODEL), F32),
            pltpu.VMEM((2, tm, D_MODEL), BF16),
            pltpu.VMEM((tm, tf), BF16),
            pltpu.VMEM((N_KV_HEADS, 2 * WINDOW, KV_WIDTH), BF16),
            pltpu.VMEM((2 * WINDOW, KV_WIDTH), BF16),
            pltpu.VMEM((CONV_HALO + BLOCK, CONV_WIDTH), F32),
            pltpu.VMEM((2, CONV_HALO, CONV_WIDTH), F32),
            pltpu.VMEM((SUBLANES - 1, CONV_HALO + BLOCK - SUBLANES, CONV_WIDTH), F32),
            pltpu.VMEM((CONV_KERNEL, SUBLANES, CONV_WIDTH), F32),
            pltpu.VMEM((SGU_HEADS // 2, 2 * CHUNK, CHUNK), BF16),
            pltpu.VMEM((KV_WIDTH, KV_WIDTH), BF16),
            pltpu.VMEM((2, WINDOW, 2 * WINDOW), F32),
        ],
        compiler_params=pltpu.CompilerParams(
            dimension_semantics=("arbitrary", "arbitrary"), vmem_limit_bytes=VMEM_LIMIT),
        name="layer_tail",
    )(proj, x2, gq, gk, sinks, cw, cb, clg, clb, slg, slb, sw, sb, og, wo, g2, wu, wd)


def _attn_perm():
    perm = np.empty((ATTN_WIDTH,), np.int32)
    for p in range(N_KV_HEADS // 2):
        for g in range(GROUP):
            for e in range(2):
                head = GROUP * (2 * p + e) + g
                dst = (GROUP * p + g) * LANES + e * HALF
                perm[dst:dst + HEAD_DIM] = np.arange(head * HEAD_DIM, (head + 1) * HEAD_DIM)
    return perm


def _pick_tile(n, pref):
    while n % pref:
        pref //= 2
    return pref


def kernel(x, ln1_g, w_in, q_norm_g, k_norm_g, sinks, conv_w, conv_b, conv_ln_g, conv_ln_b,
           sgu_ln_g, sgu_ln_b, sgu_w, sgu_b, out_norm_g, w_out, ln2_g, w_up, w_down):
    B, S, D = x.shape
    depth = w_in.shape[0]
    assert D == D_MODEL and S % WINDOW == 0
    T = B * S
    tm_proj = _pick_tile(T, 512)
    tm_tail = _pick_tile(T, 512)
    tf = D_FF // (2 * (tm_tail // BLOCK))
    perm = np.concatenate([_attn_perm(), np.arange(ATTN_WIDTH, D_MODEL, dtype=np.int32)])

    x2 = x.reshape(T, D)
    for l in range(depth):
        proj = _in_proj(x2, ln1_g[l][None, :], _cast_rows(w_in, l, 256, name="cast_w_in"), tm_proj)
        x2 = _layer_tail(
            proj, x2, B, S, tm_tail, tf,
            jnp.tile(q_norm_g[l], N_Q_HEADS)[None, :], jnp.tile(k_norm_g[l], N_KV_HEADS)[None, :],
            sinks[l], conv_w[l], conv_b[l][None, :], conv_ln_g[l][None, :], conv_ln_b[l][None, :],
            sgu_ln_g[l][None, :], sgu_ln_b[l][None, :],
            sgu_w[l].reshape(SGU_HEADS // 2, 2 * CHUNK, CHUNK),
            jnp.repeat(sgu_b[l].T, HEAD_DIM, axis=1),
            out_norm_g[l][perm][None, :],
            _cast_rows(w_out, l, HEAD_DIM, _attn_src_head, name="cast_w_out"), ln2_g[l][None, :],
            _cast_column_chunks(w_up, l, tf), _cast_rows(w_down, l, 512, name="cast_w_down"))
    return x2.reshape(B, S, D)
```

```python
import functools

import jax
import jax.numpy as jnp
import numpy as np
from jax import lax
from jax.experimental import pallas as pl
from jax.experimental.pallas import tpu as pltpu

F32 = jnp.float32
BF16 = jnp.bfloat16

D_MODEL = 2048
HEAD_DIM = 64
ATTN_WIDTH = 1024
N_Q_HEADS = 16
N_KV_HEADS = 4
GROUP = N_Q_HEADS // N_KV_HEADS
KV_WIDTH = 256
WINDOW = 128
CONV_WIDTH = 512
CONV_KERNEL = 31
CONV_HALO = 32
SGU_WIDTH = 512
SGU_HEADS = 8
CHUNK = 128
D_FF = 4 * D_MODEL
D_IN = ATTN_WIDTH + 2 * KV_WIDTH + 2 * CONV_WIDTH + 2 * SGU_WIDTH
EPS = 1e-6
NEG_INF = -1e30
LANES = 128
HALF = LANES // 2
SUBLANES = 8
SOFTMAX_ROWS = 64
CONV_ROWS = 32
BLOCK = WINDOW
MXU_TILE = 256
_DONE = object()
CONV_TAPS_PER_PIECE = 8
ATTENTION_PIECES = 5 + 2 * N_KV_HEADS
CONV_PIECES = 1 + (BLOCK // CONV_ROWS) * -(-CONV_KERNEL // CONV_TAPS_PER_PIECE)
MLP_PIECES = 16
OUT_PROJ_PIECES = 2 * (D_MODEL // MXU_TILE)
SPLIT_K = ATTN_WIDTH
SPLIT_V = SPLIT_K + KV_WIDTH
SPLIT_CONV = SPLIT_V + KV_WIDTH
SPLIT_SGU = SPLIT_CONV + 2 * CONV_WIDTH

VMEM_LIMIT = 60 * 1024 * 1024


def _rms_scale(x):
    return lax.rsqrt(jnp.mean(x * x, axis=-1, keepdims=True) + EPS)


def _layer_norm(x, g, b):
    mu = jnp.mean(x, axis=-1, keepdims=True)
    xc = x - mu
    return xc * lax.rsqrt(jnp.mean(xc * xc, axis=-1, keepdims=True) + EPS) * g + b


def _cast_kernel(w_ref, o_ref):
    o_ref[...] = w_ref[...].astype(BF16)


def _cast_rows(w, l, block_rows, name):
    _, K, N = w.shape
    return pl.pallas_call(
        _cast_kernel,
        grid=(K // block_rows,),
        in_specs=[pl.BlockSpec((None, block_rows, N), lambda r: (l, r, 0))],
        out_specs=pl.BlockSpec((block_rows, N), lambda r: (r, 0)),
        out_shape=jax.ShapeDtypeStruct((K, N), BF16),
        compiler_params=pltpu.CompilerParams(
            dimension_semantics=("arbitrary",), vmem_limit_bytes=VMEM_LIMIT),
        name=name,
    )(w)


def _cast_column_chunks(w, l, tf):
    _, K, N = w.shape
    return pl.pallas_call(
        _cast_kernel,
        grid=(N // tf,),
        in_specs=[pl.BlockSpec((None, K, tf), lambda c: (l, 0, c))],
        out_specs=pl.BlockSpec((None, K, tf), lambda c: (c, 0, 0)),
        out_shape=jax.ShapeDtypeStruct((N // tf, K, tf), BF16),
        compiler_params=pltpu.CompilerParams(
            dimension_semantics=("arbitrary",), vmem_limit_bytes=VMEM_LIMIT),
        name="cast_column_chunks",
    )(w)


W_OUT_CAST_ROWS = 2 * GROUP * HEAD_DIM


def _cast_w_out_kernel(w_ref, o_ref):
    c = pl.program_id(0)

    @pl.when(c < ATTN_WIDTH // W_OUT_CAST_ROWS)
    def _():
        for g in range(GROUP):
            for e in range(2):
                dst = (2 * g + e) * HEAD_DIM
                src = (GROUP * e + g) * HEAD_DIM
                o_ref[dst:dst + HEAD_DIM, :] = w_ref[src:src + HEAD_DIM, :].astype(BF16)

    @pl.when(c >= ATTN_WIDTH // W_OUT_CAST_ROWS)
    def _():
        o_ref[...] = w_ref[...].astype(BF16)


def _cast_w_out(w, l):
    _, K, N = w.shape
    return pl.pallas_call(
        _cast_w_out_kernel,
        grid=(K // W_OUT_CAST_ROWS,),
        in_specs=[pl.BlockSpec((None, W_OUT_CAST_ROWS, N), lambda c: (l, c, 0))],
        out_specs=pl.BlockSpec((W_OUT_CAST_ROWS, N), lambda c: (c, 0)),
        out_shape=jax.ShapeDtypeStruct((K, N), BF16),
        compiler_params=pltpu.CompilerParams(
            dimension_semantics=("arbitrary",), vmem_limit_bytes=VMEM_LIMIT),
        name="cast_w_out",
    )(w)


def _in_proj_kernel(x_ref, g_ref, w_ref, o_ref):
    x = x_ref[...]
    h = (x * _rms_scale(x) * g_ref[...]).astype(BF16)
    o_ref[...] = jnp.dot(h, w_ref[...], preferred_element_type=F32)


def _in_proj(x2, g, w, tm):
    T = x2.shape[0]
    return pl.pallas_call(
        _in_proj_kernel,
        grid=(T // tm,),
        in_specs=[
            pl.BlockSpec((tm, D_MODEL), lambda i: (i, 0)),
            pl.BlockSpec((1, D_MODEL), lambda i: (0, 0)),
            pl.BlockSpec((D_MODEL, D_IN), lambda i: (0, 0), pipeline_mode=pl.Buffered(1)),
        ],
        out_specs=pl.BlockSpec((tm, D_IN), lambda i: (i, 0)),
        out_shape=jax.ShapeDtypeStruct((T, D_IN), F32),
        compiler_params=pltpu.CompilerParams(
            dimension_semantics=("arbitrary",), vmem_limit_bytes=VMEM_LIMIT),
        name="in_proj",
    )(x2, g, w)


def _head_sumsq(x, ones_bd):
    x2 = x * x
    hi = x2.astype(BF16)
    lo = (x2 - hi.astype(F32)).astype(BF16)
    return (jnp.dot(hi, ones_bd, preferred_element_type=F32)
            + jnp.dot(lo, ones_bd, preferred_element_type=F32))


def _mixers_init(cw_ref, sw_ref, w8_ref, wsg_ref, ones_ref, mask_ref):
    r256 = lax.broadcasted_iota(jnp.int32, (KV_WIDTH, KV_WIDTH), 0)
    c256 = lax.broadcasted_iota(jnp.int32, (KV_WIDTH, KV_WIDTH), 1)
    ones_ref[...] = jnp.where((r256 // HEAD_DIM) == (c256 // HEAD_DIM), 1.0, 0.0).astype(BF16)
    for j in range(CONV_KERNEL):
        w8_ref[j] = jnp.broadcast_to(cw_ref[j:j + 1, :], (SUBLANES, CONV_WIDTH))
    ri = lax.broadcasted_iota(jnp.int32, (2 * CHUNK, CHUNK), 0) % CHUNK
    cj = lax.broadcasted_iota(jnp.int32, (2 * CHUNK, CHUNK), 1)
    for p in range(SGU_HEADS // 2):
        wsg_ref[p] = jnp.where(cj <= ri, sw_ref[p], 0.0).astype(BF16)
    qi = lax.broadcasted_iota(jnp.int32, (WINDOW, 2 * WINDOW), 0)
    sj = lax.broadcasted_iota(jnp.int32, (WINDOW, 2 * WINDOW), 1)
    rel = qi + WINDOW - sj
    mask_ref[1] = jnp.where((rel >= 0) & (rel < WINDOW), 1.0, 0.0)


def _mixers_prelude(t, krep_ref, vwin_ref, tail_ref, mask_ref):
    @pl.when(t == 0)
    def _():
        krep_ref[...] = jnp.zeros_like(krep_ref)
        vwin_ref[...] = jnp.zeros_like(vwin_ref)
        tail_ref[1] = jnp.zeros((CONV_HALO, CONV_WIDTH), F32)

    qi = lax.broadcasted_iota(jnp.int32, (WINDOW, 2 * WINDOW), 0)
    sj = (lax.broadcasted_iota(jnp.int32, (WINDOW, 2 * WINDOW), 1) + WINDOW) % (2 * WINDOW)
    rel = qi + WINDOW - sj
    mask_ref[0] = jnp.where((rel >= 0) & (rel < WINDOW) & (sj >= (1 - jnp.minimum(t, 1)) * WINDOW),
                            1.0, 0.0)


def _exact_zero(v):
    bits = lax.bitcast_convert_type(v, jnp.int32)
    return lax.shift_right_logical(lax.shift_right_logical(bits, 16), 16).astype(F32)


def _low_half():
    return lax.broadcasted_iota(jnp.int32, (WINDOW, LANES), 1) < HALF


def _attention_pieces(parity, q_ref, k_ref, v_ref, gq_ref, gk_ref, sinks_ref, og_ref, o_ref,
                      krep_ref, vwin_ref, ones_ref, mask_ref):
    low_half = _low_half()
    ones_bd = ones_ref[...]
    colblk = lax.broadcasted_iota(jnp.int32, (WINDOW, KV_WIDTH), 1) // HEAD_DIM
    gq = gq_ref[...] * (HEAD_DIM ** -0.5)
    k = k_ref[...]
    kn = k * lax.rsqrt(_head_sumsq(k, ones_bd) * (1.0 / HEAD_DIM) + EPS) * gk_ref[...]
    cur = pl.ds(pl.multiple_of(parity * WINDOW, WINDOW), WINDOW)
    vwin_ref[cur, :] = v_ref[...].astype(BF16)
    for p in range(N_KV_HEADS // 2):
        blk = kn[:, p * LANES:(p + 1) * LANES]
        swapped = pltpu.roll(blk, HALF, 1)
        even = jnp.where(low_half, blk, swapped).astype(BF16)
        odd = jnp.where(low_half, swapped, blk).astype(BF16)
        krep_ref[2 * p, cur, :] = jnp.concatenate([even, even], axis=1)
        krep_ref[2 * p + 1, cur, :] = jnp.concatenate([odd, odd], axis=1)

    yield
    lhs = []
    for kh in range(N_KV_HEADS):
        qs = q_ref[:, kh * KV_WIDTH:(kh + 1) * KV_WIDTH]
        qn = (qs * lax.rsqrt(_head_sumsq(qs, ones_bd) * (1.0 / HEAD_DIM) + EPS)
              * gq[:, kh * KV_WIDTH:(kh + 1) * KV_WIDTH]).astype(BF16)
        lhs.append(jnp.concatenate(
            [jnp.where(colblk == g, qn, jnp.zeros_like(qn)) for g in range(GROUP)], axis=0))
    yield
    all_logits = [lax.dot_general(lhs[kh], krep_ref[kh], (((1,), (1,)), ((), ())),
                                  preferred_element_type=F32)
                  for kh in range(N_KV_HEADS)]
    yield
    all_probs = []
    for kh in range(N_KV_HEADS):
        logits = all_logits[kh]
        probs = []
        for r0 in range(0, GROUP * WINDOW, SOFTMAX_ROWS):
            q0 = r0 % WINDOW
            valid = mask_ref[parity, q0:q0 + SOFTMAX_ROWS, :] > 0.5
            lg = jnp.where(valid, logits[r0:r0 + SOFTMAX_ROWS], NEG_INF)
            sink = sinks_ref[kh * GROUP + r0 // WINDOW]
            m = jnp.maximum(jnp.max(lg, axis=-1, keepdims=True), sink)
            pexp = jnp.exp(lg - m)
            denom = jnp.sum(pexp, axis=-1, keepdims=True) + jnp.exp(sink - m)
            probs.append((pexp * (1.0 / denom)).astype(BF16))
            if (r0 + SOFTMAX_ROWS) % (GROUP * WINDOW // 2) == 0:
                yield
        all_probs.append(jnp.concatenate(probs, axis=0))
    outs = [jnp.dot(all_probs[kh], vwin_ref[...], preferred_element_type=F32)
            for kh in range(N_KV_HEADS)]
    yield
    y = jnp.concatenate(
        [jnp.where(low_half,
                   outs[2 * p][g * WINDOW:(g + 1) * WINDOW, p * LANES:(p + 1) * LANES],
                   outs[2 * p + 1][g * WINDOW:(g + 1) * WINDOW, p * LANES:(p + 1) * LANES])
         for p in range(N_KV_HEADS // 2) for g in range(GROUP)], axis=1)
    o_ref[:, 0:ATTN_WIDTH] = (y * _rms_scale(y) * og_ref[:, 0:ATTN_WIDTH]).astype(BF16)


def _conv_pieces(parity, ca_ref, cg_ref, cb_ref, clg_ref, clb_ref, og_ref, o_ref,
                 hbuf_ref, tail_ref, hs_ref, w8_ref, anchors):
    hbuf_ref[0:CONV_HALO, :] = tail_ref[1 - parity]
    hbuf_ref[CONV_HALO:CONV_HALO + BLOCK, :] = ca_ref[...] * jax.nn.sigmoid(cg_ref[...])
    tail_ref[parity] = hbuf_ref[BLOCK:BLOCK + CONV_HALO, :]
    for r in range(1, SUBLANES):
        hs_ref[r - 1] = hbuf_ref[r:r + BLOCK + CONV_HALO - SUBLANES, :]
    groups = CONV_ROWS // SUBLANES
    for r0 in range(0, BLOCK, CONV_ROWS):
        yield
        accs = [None] * groups
        for j in range(CONV_KERNEL):
            if j % CONV_TAPS_PER_PIECE == 0 and j:
                yield
            a, r = divmod(CONV_HALO - (CONV_KERNEL - 1) + j, SUBLANES)
            w8 = w8_ref[j]
            for g in range(groups):
                base = r0 + SUBLANES * (a + g)
                if r == 0:
                    src = hbuf_ref[base:base + SUBLANES, :]
                else:
                    src = hs_ref[r - 1, base:base + SUBLANES, :]
                accs[g] = src * w8 if accs[g] is None else accs[g] + src * w8
        acc = jnp.concatenate(accs, axis=0)
        hc = _layer_norm(acc + cb_ref[...], clg_ref[...], clb_ref[...])
        yc = hc * jax.nn.sigmoid(hc)
        o_ref[r0:r0 + CONV_ROWS, ATTN_WIDTH:ATTN_WIDTH + CONV_WIDTH] = (
            yc * _rms_scale(yc) * og_ref[:, ATTN_WIDTH:ATTN_WIDTH + CONV_WIDTH]).astype(BF16)
        anchors.append(yc[0:2 * SUBLANES, 0:LANES])


def _sgu_pieces(su_ref, sv_ref, slg_ref, slb_ref, sb_ref, og_ref, o_ref, wsg_ref):
    low_half = _low_half()
    vn = _layer_norm(sv_ref[...], slg_ref[...], slb_ref[...]).astype(BF16)
    ys = []
    for p in range(SGU_HEADS // 2):
        r = jnp.dot(wsg_ref[p], vn[:, p * LANES:(p + 1) * LANES],
                    preferred_element_type=F32)
        sp = jnp.where(low_half, r[0:CHUNK], r[CHUNK:2 * CHUNK])
        sp = sp + sb_ref[:, p * LANES:(p + 1) * LANES]
        ys.append(su_ref[:, p * LANES:(p + 1) * LANES] * sp)
    ysg = jnp.concatenate(ys, axis=1)
    o_ref[:, ATTN_WIDTH + CONV_WIDTH:] = (
        ysg * _rms_scale(ysg) * og_ref[:, ATTN_WIDTH + CONV_WIDTH:]).astype(BF16)
    yield


def _layer_tail_kernel(proj_ref, x_ref,
                       gq_ref, gk_ref, sinks_ref, cw_ref, cb_ref, clg_ref, clb_ref,
                       slg_ref, slb_ref, sw_ref, sb_ref, og_ref,
                       wo_ref, g2_ref, wu_hbm, wd_hbm,
                       o_ref,
                       wu_buf, wd_buf, w_sem, mix_ref, x1_ref, h_ref, uu_ref,
                       krep_ref, vwin_ref, hbuf_ref, tail_ref, hs_ref, w8_ref, wsg_ref, ones_ref,
                       mask_ref, *, n_tiles, blocks_per_tile, blocks_per_seq):
    i = pl.program_id(0)
    g = pl.program_id(1)
    n_g = pl.num_programs(1)
    n_chunks = 2 * n_g
    tf = wu_buf.shape[2]
    first = i == 0
    last = i == n_tiles
    final = last & (g == n_g - 1)
    blk = jnp.minimum(i, n_tiles - 1) * blocks_per_tile + g
    rows = pl.ds(pl.multiple_of(g * BLOCK, BLOCK), BLOCK)
    h_mlp = (i + 1) % 2
    h_new = i % 2

    def weight_copies(chunk, slot):
        return (pltpu.make_async_copy(wu_hbm.at[chunk], wu_buf.at[slot], w_sem.at[0, slot]),
                pltpu.make_async_copy(wd_hbm.at[pl.ds(pl.multiple_of(chunk * tf, tf), tf)],
                                      wd_buf.at[slot], w_sem.at[1, slot]))

    def start_weights(chunk, slot):
        for copy in weight_copies(chunk, slot):
            copy.start()

    def wait_weights(slot):
        for copy in weight_copies(0, slot):
            copy.wait()

    @pl.when(first & (g == 0))
    def _():
        _mixers_init(cw_ref, sw_ref, w8_ref, wsg_ref, ones_ref, mask_ref)
        start_weights(0, 0)
        start_weights(1, 1)

    @pl.when((g == 0) & jnp.logical_not(first))
    def _():
        o_ref[...] = x1_ref[...]

    def mlp_chunk(slot, anchors=()):
        wu_ref, wd_ref = wu_buf.at[slot], wd_buf.at[slot]
        kh = D_MODEL // 2
        for n in range(0, tf, MXU_TILE):
            u = jnp.dot(h_ref[h_mlp, :, 0:kh], wu_ref[0:kh, n:n + MXU_TILE],
                        preferred_element_type=F32)
            yield
            u = u + jnp.dot(h_ref[h_mlp, :, kh:], wu_ref[kh:, n:n + MXU_TILE],
                            preferred_element_type=F32)
            u = jnp.maximum(u, 0.0)
            uu_ref[:, n:n + MXU_TILE] = (u * u).astype(BF16)
            yield
        for m in range(0, D_MODEL, MXU_TILE):
            while anchors:
                uu_ref[0:2 * SUBLANES, 0:LANES] += _exact_zero(anchors.pop()).astype(BF16)
            o_ref[:, m:m + MXU_TILE] += jnp.dot(uu_ref[...], wd_ref[:, m:m + MXU_TILE],
                                                preferred_element_type=F32)
            yield

    t_seq = blk % blocks_per_seq
    parity = t_seq % 2
    mix = mix_ref.at[rows]

    def col(a, w):
        return proj_ref.at[:, a:a + w]

    def attention_streams():
        _mixers_prelude(t_seq, krep_ref, vwin_ref, tail_ref, mask_ref)
        x1_ref[rows, :] = x_ref[...]
        return [(_attention_pieces(parity, col(0, ATTN_WIDTH), col(SPLIT_K, KV_WIDTH),
                                   col(SPLIT_V, KV_WIDTH), gq_ref, gk_ref, sinks_ref, og_ref, mix,
                                   krep_ref, vwin_ref, ones_ref, mask_ref), ATTENTION_PIECES)]

    def conv_sgu_streams(anchors):
        return [
            (_conv_pieces(parity, col(SPLIT_CONV, CONV_WIDTH),
                          col(SPLIT_CONV + CONV_WIDTH, CONV_WIDTH), cb_ref, clg_ref, clb_ref,
                          og_ref, mix, hbuf_ref, tail_ref, hs_ref, w8_ref, anchors), CONV_PIECES),
            (_sgu_pieces(col(SPLIT_SGU, SGU_WIDTH), col(SPLIT_SGU + SGU_WIDTH, SGU_WIDTH),
                         slg_ref, slb_ref, sb_ref, og_ref, mix, wsg_ref), 1)]

    def out_proj_pieces():
        ssq = None
        for n in range(0, D_MODEL, MXU_TILE):
            x1 = x1_ref[:, n:n + MXU_TILE] + jnp.dot(mix_ref[...], wo_ref[:, n:n + MXU_TILE],
                                                     preferred_element_type=F32)
            x1_ref[:, n:n + MXU_TILE] = x1
            part = jnp.sum(x1 * x1, axis=-1, keepdims=True)
            ssq = part if ssq is None else ssq + part
            yield
        scale = lax.rsqrt(ssq * (1.0 / D_MODEL) + EPS)
        for n in range(0, D_MODEL, MXU_TILE):
            h_ref[h_new, :, n:n + MXU_TILE] = (
                x1_ref[:, n:n + MXU_TILE] * scale * g2_ref[:, n:n + MXU_TILE]).astype(BF16)
            yield

    def interleave(streams):
        done = [0] * len(streams)
        live = set(range(len(streams)))
        while live:
            s = min(live, key=lambda j: (done[j] + 1) / streams[j][1])
            done[s] += 1
            if next(streams[s][0], _DONE) is _DONE:
                live.remove(s)

    middle = jnp.logical_not(first | last)
    is_proj = g == n_g - 1

    wait_weights(0)

    @pl.when(middle)
    def _():
        interleave(attention_streams() + [(mlp_chunk(0), MLP_PIECES)])

    @pl.when(first)
    def _():
        interleave(attention_streams())

    @pl.when(last)
    def _():
        interleave([(mlp_chunk(0), MLP_PIECES)])

    @pl.when(jnp.logical_not(final))
    def _():
        start_weights((2 * g + 2) % n_chunks, 0)

    wait_weights(1)

    @pl.when(middle)
    def _():
        anchors = []
        interleave(conv_sgu_streams(anchors) + [(mlp_chunk(1, anchors), MLP_PIECES)])

    @pl.when(first)
    def _():
        interleave(conv_sgu_streams([]))

    @pl.when(last)
    def _():
        interleave([(mlp_chunk(1), MLP_PIECES)])

    @pl.when(is_proj & jnp.logical_not(last))
    def _():
        interleave([(out_proj_pieces(), OUT_PROJ_PIECES)])

    @pl.when(jnp.logical_not(final))
    def _():
        start_weights((2 * g + 3) % n_chunks, 1)


def _layer_tail(proj, x2, B, S, tm, tf, gq, gk, sinks, cw, cb, clg, clb, slg, slb, sw, sb, og,
                wo, g2, wu, wd):
    T = B * S
    n_tiles = T // tm
    bpt = tm // BLOCK
    assert D_FF // tf == 2 * bpt, "each mixer block step runs two MLP chunks"
    assert S % BLOCK == 0 and T % tm == 0

    def blk(i, g):
        return jnp.minimum(i, n_tiles - 1) * bpt + g

    def whole(shape, **kw):
        return pl.BlockSpec(shape, lambda i, g: (0,) * len(shape), **kw)

    return pl.pallas_call(
        functools.partial(_layer_tail_kernel, n_tiles=n_tiles, blocks_per_tile=bpt,
                          blocks_per_seq=S // BLOCK),
        grid=(n_tiles + 1, bpt),
        in_specs=[
            pl.BlockSpec((BLOCK, D_IN), lambda i, g: (blk(i, g), 0)),
            pl.BlockSpec((BLOCK, D_MODEL), lambda i, g: (blk(i, g), 0)),
            whole((1, ATTN_WIDTH)), whole((1, KV_WIDTH)),
            pl.BlockSpec(memory_space=pltpu.MemorySpace.SMEM),
            whole((CONV_KERNEL, CONV_WIDTH)), whole((1, CONV_WIDTH)),
            whole((1, CONV_WIDTH)), whole((1, CONV_WIDTH)),
            whole((1, SGU_WIDTH)), whole((1, SGU_WIDTH)),
            whole((SGU_HEADS // 2, 2 * CHUNK, CHUNK)), whole((CHUNK, SGU_WIDTH)),
            whole((1, D_MODEL)),
            whole((D_MODEL, D_MODEL), pipeline_mode=pl.Buffered(1)),
            whole((1, D_MODEL)),
            pl.BlockSpec(memory_space=pl.ANY),
            pl.BlockSpec(memory_space=pl.ANY),
        ],
        out_specs=pl.BlockSpec((tm, D_MODEL), lambda i, g: (jnp.maximum(i - 1, 0), 0)),
        out_shape=jax.ShapeDtypeStruct((T, D_MODEL), F32),
        scratch_shapes=[
            pltpu.VMEM((2, D_MODEL, tf), BF16),
            pltpu.VMEM((2, tf, D_MODEL), BF16),
            pltpu.SemaphoreType.DMA((2, 2)),
            pltpu.VMEM((tm, D_MODEL), BF16),
            pltpu.VMEM((tm, D_MODEL), F32),
            pltpu.VMEM((2, tm, D_MODEL), BF16),
            pltpu.VMEM((tm, tf), BF16),
            pltpu.VMEM((N_KV_HEADS, 2 * WINDOW, KV_WIDTH), BF16),
            pltpu.VMEM((2 * WINDOW, KV_WIDTH), BF16),
            pltpu.VMEM((CONV_HALO + BLOCK, CONV_WIDTH), F32),
            pltpu.VMEM((2, CONV_HALO, CONV_WIDTH), F32),
            pltpu.VMEM((SUBLANES - 1, CONV_HALO + BLOCK - SUBLANES, CONV_WIDTH), F32),
            pltpu.VMEM((CONV_KERNEL, SUBLANES, CONV_WIDTH), F32),
            pltpu.VMEM((SGU_HEADS // 2, 2 * CHUNK, CHUNK), BF16),
            pltpu.VMEM((KV_WIDTH, KV_WIDTH), BF16),
            pltpu.VMEM((2, WINDOW, 2 * WINDOW), F32),
        ],
        compiler_params=pltpu.CompilerParams(
            dimension_semantics=("arbitrary", "arbitrary"), vmem_limit_bytes=VMEM_LIMIT),
        name="layer_tail",
    )(proj, x2, gq, gk, sinks, cw, cb, clg, clb, slg, slb, sw, sb, og, wo, g2, wu, wd)


def _attn_perm():
    perm = np.empty((ATTN_WIDTH,), np.int32)
    for p in range(N_KV_HEADS // 2):
        for g in range(GROUP):
            for e in range(2):
                head = GROUP * (2 * p + e) + g
                dst = (GROUP * p + g) * LANES + e * HALF
                perm[dst:dst + HEAD_DIM] = np.arange(head * HEAD_DIM, (head + 1) * HEAD_DIM)
    return perm


def _pick_tile(n, pref):
    while n % pref:
        pref //= 2
    return pref


def kernel(x, ln1_g, w_in, q_norm_g, k_norm_g, sinks, conv_w, conv_b, conv_ln_g, conv_ln_b,
           sgu_ln_g, sgu_ln_b, sgu_w, sgu_b, out_norm_g, w_out, ln2_g, w_up, w_down):
    B, S, D = x.shape
    depth = w_in.shape[0]
    assert D == D_MODEL and S % WINDOW == 0
    T = B * S
    tm_proj = _pick_tile(T, 512)
    tm_tail = _pick_tile(T, 512)
    tf = D_FF // (2 * (tm_tail // BLOCK))
    perm = np.concatenate([_attn_perm(), np.arange(ATTN_WIDTH, D_MODEL, dtype=np.int32)])

    x2 = x.reshape(T, D)
    for l in range(depth):
        proj = _in_proj(x2, ln1_g[l][None, :], _cast_rows(w_in, l, 256, name="cast_w_in"), tm_proj)
        x2 = _layer_tail(
            proj, x2, B, S, tm_tail, tf,
            jnp.tile(q_norm_g[l], N_Q_HEADS)[None, :], jnp.tile(k_norm_g[l], N_KV_HEADS)[None, :],
            sinks[l], conv_w[l], conv_b[l][None, :], conv_ln_g[l][None, :], conv_ln_b[l][None, :],
            sgu_ln_g[l][None, :], sgu_ln_b[l][None, :],
            sgu_w[l].reshape(SGU_HEADS // 2, 2 * CHUNK, CHUNK),
            jnp.repeat(sgu_b[l].T, HEAD_DIM, axis=1),
            out_norm_g[l][perm][None, :],
            _cast_w_out(w_out, l), ln2_g[l][None, :],
            _cast_column_chunks(w_up, l, tf), _cast_rows(w_down, l, 512, name="cast_w_down"))
    return x2.reshape(B, S, D)
```

```python
import functools

import jax
import jax.numpy as jnp
import numpy as np
from jax import lax
from jax.experimental import pallas as pl
from jax.experimental.pallas import tpu as pltpu

F32 = jnp.float32
BF16 = jnp.bfloat16

D_MODEL = 2048
HEAD_DIM = 64
ATTN_WIDTH = 1024
N_Q_HEADS = 16
N_KV_HEADS = 4
GROUP = N_Q_HEADS // N_KV_HEADS
KV_WIDTH = 256
WINDOW = 128
CONV_WIDTH = 512
CONV_KERNEL = 31
CONV_HALO = 32
SGU_WIDTH = 512
SGU_HEADS = 8
CHUNK = 128
D_FF = 4 * D_MODEL
D_IN = ATTN_WIDTH + 2 * KV_WIDTH + 2 * CONV_WIDTH + 2 * SGU_WIDTH
EPS = 1e-6
NEG_INF = -1e30
LANES = 128
HALF = LANES // 2
SUBLANES = 8
SOFTMAX_ROWS = 64
CONV_ROWS = 32
BLOCK = WINDOW
MXU_TILE = 256
_DONE = object()
CONV_TAPS_PER_PIECE = 8
ATTENTION_PIECES = 5 + 2 * N_KV_HEADS
CONV_PIECES = 1 + (BLOCK // CONV_ROWS) * -(-CONV_KERNEL // CONV_TAPS_PER_PIECE)
MLP_PIECES = 16
OUT_PROJ_PIECES = D_MODEL // MXU_TILE
SPLIT_K = ATTN_WIDTH
SPLIT_V = SPLIT_K + KV_WIDTH
SPLIT_CONV = SPLIT_V + KV_WIDTH
SPLIT_SGU = SPLIT_CONV + 2 * CONV_WIDTH

VMEM_LIMIT = 60 * 1024 * 1024


def _rms_scale(x):
    return lax.rsqrt(jnp.mean(x * x, axis=-1, keepdims=True) + EPS)


def _layer_norm(x, g, b):
    mu = jnp.mean(x, axis=-1, keepdims=True)
    xc = x - mu
    return xc * lax.rsqrt(jnp.mean(xc * xc, axis=-1, keepdims=True) + EPS) * g + b


def _cast_kernel(w_ref, o_ref):
    o_ref[...] = w_ref[...].astype(BF16)


def _cast_rows(w, l, block_rows, name):
    _, K, N = w.shape
    return pl.pallas_call(
        _cast_kernel,
        grid=(K // block_rows,),
        in_specs=[pl.BlockSpec((None, block_rows, N), lambda r: (l, r, 0))],
        out_specs=pl.BlockSpec((block_rows, N), lambda r: (r, 0)),
        out_shape=jax.ShapeDtypeStruct((K, N), BF16),
        compiler_params=pltpu.CompilerParams(
            dimension_semantics=("arbitrary",), vmem_limit_bytes=VMEM_LIMIT),
        name=name,
    )(w)


def _cast_column_chunks(w, l, tf):
    _, K, N = w.shape
    return pl.pallas_call(
        _cast_kernel,
        grid=(N // tf,),
        in_specs=[pl.BlockSpec((None, K, tf), lambda c: (l, 0, c))],
        out_specs=pl.BlockSpec((None, K, tf), lambda c: (c, 0, 0)),
        out_shape=jax.ShapeDtypeStruct((N // tf, K, tf), BF16),
        compiler_params=pltpu.CompilerParams(
            dimension_semantics=("arbitrary",), vmem_limit_bytes=VMEM_LIMIT),
        name="cast_column_chunks",
    )(w)


W_OUT_CAST_ROWS = 2 * GROUP * HEAD_DIM


def _cast_w_out_kernel(w_ref, o_ref):
    c = pl.program_id(0)

    @pl.when(c < ATTN_WIDTH // W_OUT_CAST_ROWS)
    def _():
        for g in range(GROUP):
            for e in range(2):
                dst = (2 * g + e) * HEAD_DIM
                src = (GROUP * e + g) * HEAD_DIM
                o_ref[dst:dst + HEAD_DIM, :] = w_ref[src:src + HEAD_DIM, :].astype(BF16)

    @pl.when(c >= ATTN_WIDTH // W_OUT_CAST_ROWS)
    def _():
        o_ref[...] = w_ref[...].astype(BF16)


def _cast_w_out(w, l):
    _, K, N = w.shape
    return pl.pallas_call(
        _cast_w_out_kernel,
        grid=(K // W_OUT_CAST_ROWS,),
        in_specs=[pl.BlockSpec((None, W_OUT_CAST_ROWS, N), lambda c: (l, c, 0))],
        out_specs=pl.BlockSpec((W_OUT_CAST_ROWS, N), lambda c: (c, 0)),
        out_shape=jax.ShapeDtypeStruct((K, N), BF16),
        compiler_params=pltpu.CompilerParams(
            dimension_semantics=("arbitrary",), vmem_limit_bytes=VMEM_LIMIT),
        name="cast_w_out",
    )(w)


def _in_proj_kernel(x_ref, g_ref, w_ref, o_ref):
    x = x_ref[...]
    h = (x * g_ref[...]).astype(BF16)
    o_ref[...] = jnp.dot(h, w_ref[...], preferred_element_type=F32) * _rms_scale(x)


def _in_proj(x2, g, w, tm):
    T = x2.shape[0]
    return pl.pallas_call(
        _in_proj_kernel,
        grid=(T // tm,),
        in_specs=[
            pl.BlockSpec((tm, D_MODEL), lambda i: (i, 0)),
            pl.BlockSpec((1, D_MODEL), lambda i: (0, 0)),
            pl.BlockSpec((D_MODEL, D_IN), lambda i: (0, 0), pipeline_mode=pl.Buffered(1)),
        ],
        out_specs=pl.BlockSpec((tm, D_IN), lambda i: (i, 0)),
        out_shape=jax.ShapeDtypeStruct((T, D_IN), F32),
        compiler_params=pltpu.CompilerParams(
            dimension_semantics=("arbitrary",), vmem_limit_bytes=VMEM_LIMIT),
        name="in_proj",
    )(x2, g, w)


def _head_sumsq(x, ones_bd):
    x2 = x * x
    hi = x2.astype(BF16)
    lo = (x2 - hi.astype(F32)).astype(BF16)
    return (jnp.dot(hi, ones_bd, preferred_element_type=F32)
            + jnp.dot(lo, ones_bd, preferred_element_type=F32))


def _mixers_init(cw_ref, sw_ref, w8_ref, wsg_ref, ones_ref, mask_ref):
    r256 = lax.broadcasted_iota(jnp.int32, (KV_WIDTH, KV_WIDTH), 0)
    c256 = lax.broadcasted_iota(jnp.int32, (KV_WIDTH, KV_WIDTH), 1)
    ones_ref[...] = jnp.where((r256 // HEAD_DIM) == (c256 // HEAD_DIM), 1.0, 0.0).astype(BF16)
    for j in range(CONV_KERNEL):
        w8_ref[j] = jnp.broadcast_to(cw_ref[j:j + 1, :], (SUBLANES, CONV_WIDTH))
    ri = lax.broadcasted_iota(jnp.int32, (2 * CHUNK, CHUNK), 0) % CHUNK
    cj = lax.broadcasted_iota(jnp.int32, (2 * CHUNK, CHUNK), 1)
    for p in range(SGU_HEADS // 2):
        wsg_ref[p] = jnp.where(cj <= ri, sw_ref[p], 0.0).astype(BF16)
    qi = lax.broadcasted_iota(jnp.int32, (WINDOW, 2 * WINDOW), 0)
    sj = lax.broadcasted_iota(jnp.int32, (WINDOW, 2 * WINDOW), 1)
    rel = qi + WINDOW - sj
    mask_ref[1] = jnp.where((rel >= 0) & (rel < WINDOW), 1.0, 0.0)


def _mixers_prelude(t, krep_ref, vwin_ref, tail_ref, mask_ref):
    @pl.when(t == 0)
    def _():
        krep_ref[...] = jnp.zeros_like(krep_ref)
        vwin_ref[...] = jnp.zeros_like(vwin_ref)
        tail_ref[1] = jnp.zeros((CONV_HALO, CONV_WIDTH), F32)

    qi = lax.broadcasted_iota(jnp.int32, (WINDOW, 2 * WINDOW), 0)
    sj = (lax.broadcasted_iota(jnp.int32, (WINDOW, 2 * WINDOW), 1) + WINDOW) % (2 * WINDOW)
    rel = qi + WINDOW - sj
    mask_ref[0] = jnp.where((rel >= 0) & (rel < WINDOW) & (sj >= (1 - jnp.minimum(t, 1)) * WINDOW),
                            1.0, 0.0)


def _exact_zero(v):
    bits = lax.bitcast_convert_type(v, jnp.int32)
    return lax.shift_right_logical(lax.shift_right_logical(bits, 16), 16).astype(F32)


def _low_half():
    return lax.broadcasted_iota(jnp.int32, (WINDOW, LANES), 1) < HALF


def _attention_pieces(parity, q_ref, k_ref, v_ref, gq_ref, gk_ref, sinks_ref, og_ref, o_ref,
                      krep_ref, vwin_ref, ones_ref, mask_ref):
    low_half = _low_half()
    ones_bd = ones_ref[...]
    colblk = lax.broadcasted_iota(jnp.int32, (WINDOW, KV_WIDTH), 1) // HEAD_DIM
    gq = gq_ref[...] * (HEAD_DIM ** -0.5)
    k = k_ref[...]
    kn = k * lax.rsqrt(_head_sumsq(k, ones_bd) * (1.0 / HEAD_DIM) + EPS) * gk_ref[...]
    cur = pl.ds(pl.multiple_of(parity * WINDOW, WINDOW), WINDOW)
    vwin_ref[cur, :] = v_ref[...].astype(BF16)
    for p in range(N_KV_HEADS // 2):
        blk = kn[:, p * LANES:(p + 1) * LANES]
        swapped = pltpu.roll(blk, HALF, 1)
        even = jnp.where(low_half, blk, swapped).astype(BF16)
        odd = jnp.where(low_half, swapped, blk).astype(BF16)
        krep_ref[2 * p, cur, :] = jnp.concatenate([even, even], axis=1)
        krep_ref[2 * p + 1, cur, :] = jnp.concatenate([odd, odd], axis=1)

    yield
    lhs = []
    for kh in range(N_KV_HEADS):
        qs = q_ref[:, kh * KV_WIDTH:(kh + 1) * KV_WIDTH]
        qn = (qs * lax.rsqrt(_head_sumsq(qs, ones_bd) * (1.0 / HEAD_DIM) + EPS)
              * gq[:, kh * KV_WIDTH:(kh + 1) * KV_WIDTH]).astype(BF16)
        lhs.append(jnp.concatenate(
            [jnp.where(colblk == g, qn, jnp.zeros_like(qn)) for g in range(GROUP)], axis=0))
    yield
    all_logits = [lax.dot_general(lhs[kh], krep_ref[kh], (((1,), (1,)), ((), ())),
                                  preferred_element_type=F32)
                  for kh in range(N_KV_HEADS)]
    yield
    all_probs = []
    for kh in range(N_KV_HEADS):
        logits = all_logits[kh]
        probs = []
        for r0 in range(0, GROUP * WINDOW, SOFTMAX_ROWS):
            q0 = r0 % WINDOW
            valid = mask_ref[parity, q0:q0 + SOFTMAX_ROWS, :] > 0.5
            lg = jnp.where(valid, logits[r0:r0 + SOFTMAX_ROWS], NEG_INF)
            sink = sinks_ref[kh * GROUP + r0 // WINDOW]
            m = jnp.maximum(jnp.max(lg, axis=-1, keepdims=True), sink)
            pexp = jnp.exp(lg - m)
            denom = jnp.sum(pexp, axis=-1, keepdims=True) + jnp.exp(sink - m)
            probs.append((pexp * (1.0 / denom)).astype(BF16))
            if (r0 + SOFTMAX_ROWS) % (GROUP * WINDOW // 2) == 0:
                yield
        all_probs.append(jnp.concatenate(probs, axis=0))
    outs = [jnp.dot(all_probs[kh], vwin_ref[...], preferred_element_type=F32)
            for kh in range(N_KV_HEADS)]
    yield
    y = jnp.concatenate(
        [jnp.where(low_half,
                   outs[2 * p][g * WINDOW:(g + 1) * WINDOW, p * LANES:(p + 1) * LANES],
                   outs[2 * p + 1][g * WINDOW:(g + 1) * WINDOW, p * LANES:(p + 1) * LANES])
         for p in range(N_KV_HEADS // 2) for g in range(GROUP)], axis=1)
    o_ref[:, 0:ATTN_WIDTH] = (y * _rms_scale(y) * og_ref[:, 0:ATTN_WIDTH]).astype(BF16)


def _conv_pieces(parity, ca_ref, cg_ref, cb_ref, clg_ref, clb_ref, og_ref, o_ref,
                 hbuf_ref, tail_ref, hs_ref, w8_ref, anchors):
    hbuf_ref[0:CONV_HALO, :] = tail_ref[1 - parity]
    hbuf_ref[CONV_HALO:CONV_HALO + BLOCK, :] = ca_ref[...] * jax.nn.sigmoid(cg_ref[...])
    tail_ref[parity] = hbuf_ref[BLOCK:BLOCK + CONV_HALO, :]
    for r in range(1, SUBLANES):
        hs_ref[r - 1] = hbuf_ref[r:r + BLOCK + CONV_HALO - SUBLANES, :]
    groups = CONV_ROWS // SUBLANES
    for r0 in range(0, BLOCK, CONV_ROWS):
        yield
        accs = [None] * groups
        for j in range(CONV_KERNEL):
            if j % CONV_TAPS_PER_PIECE == 0 and j:
                yield
            a, r = divmod(CONV_HALO - (CONV_KERNEL - 1) + j, SUBLANES)
            w8 = w8_ref[j]
            for g in range(groups):
                base = r0 + SUBLANES * (a + g)
                if r == 0:
                    src = hbuf_ref[base:base + SUBLANES, :]
                else:
                    src = hs_ref[r - 1, base:base + SUBLANES, :]
                accs[g] = src * w8 if accs[g] is None else accs[g] + src * w8
        acc = jnp.concatenate(accs, axis=0)
        hc = _layer_norm(acc + cb_ref[...], clg_ref[...], clb_ref[...])
        yc = hc * jax.nn.sigmoid(hc)
        o_ref[r0:r0 + CONV_ROWS, ATTN_WIDTH:ATTN_WIDTH + CONV_WIDTH] = (
            yc * _rms_scale(yc) * og_ref[:, ATTN_WIDTH:ATTN_WIDTH + CONV_WIDTH]).astype(BF16)
        anchors.append(yc[0:2 * SUBLANES, 0:LANES])


def _sgu_pieces(su_ref, sv_ref, slg_ref, slb_ref, sb_ref, og_ref, o_ref, wsg_ref):
    low_half = _low_half()
    vn = _layer_norm(sv_ref[...], slg_ref[...], slb_ref[...]).astype(BF16)
    ys = []
    for p in range(SGU_HEADS // 2):
        r = jnp.dot(wsg_ref[p], vn[:, p * LANES:(p + 1) * LANES],
                    preferred_element_type=F32)
        sp = jnp.where(low_half, r[0:CHUNK], r[CHUNK:2 * CHUNK])
        sp = sp + sb_ref[:, p * LANES:(p + 1) * LANES]
        ys.append(su_ref[:, p * LANES:(p + 1) * LANES] * sp)
    ysg = jnp.concatenate(ys, axis=1)
    o_ref[:, ATTN_WIDTH + CONV_WIDTH:] = (
        ysg * _rms_scale(ysg) * og_ref[:, ATTN_WIDTH + CONV_WIDTH:]).astype(BF16)
    yield


def _layer_tail_kernel(proj_ref, x_ref,
                       gq_ref, gk_ref, sinks_ref, cw_ref, cb_ref, clg_ref, clb_ref,
                       slg_ref, slb_ref, sw_ref, sb_ref, og_ref,
                       wo_ref, g2_ref, wu_hbm, wd_hbm,
                       o_ref,
                       wu_buf, wd_buf, w_sem, mix_ref, x1_ref, h_ref, scale_ref, uu_ref,
                       krep_ref, vwin_ref, hbuf_ref, tail_ref, hs_ref, w8_ref, wsg_ref, ones_ref,
                       mask_ref, *, n_tiles, blocks_per_tile, blocks_per_seq):
    i = pl.program_id(0)
    g = pl.program_id(1)
    n_g = pl.num_programs(1)
    n_chunks = 2 * n_g
    tf = wu_buf.shape[2]
    first = i == 0
    last = i == n_tiles
    final = last & (g == n_g - 1)
    blk = jnp.minimum(i, n_tiles - 1) * blocks_per_tile + g
    rows = pl.ds(pl.multiple_of(g * BLOCK, BLOCK), BLOCK)
    h_mlp = (i + 1) % 2
    h_new = i % 2

    def weight_copies(chunk, slot):
        return (pltpu.make_async_copy(wu_hbm.at[chunk], wu_buf.at[slot], w_sem.at[0, slot]),
                pltpu.make_async_copy(wd_hbm.at[pl.ds(pl.multiple_of(chunk * tf, tf), tf)],
                                      wd_buf.at[slot], w_sem.at[1, slot]))

    def start_weights(chunk, slot):
        for copy in weight_copies(chunk, slot):
            copy.start()

    def wait_weights(slot):
        for copy in weight_copies(0, slot):
            copy.wait()

    @pl.when(first & (g == 0))
    def _():
        _mixers_init(cw_ref, sw_ref, w8_ref, wsg_ref, ones_ref, mask_ref)
        start_weights(0, 0)
        start_weights(1, 1)

    @pl.when((g == 0) & jnp.logical_not(first))
    def _():
        o_ref[...] = x1_ref[...]

    def mlp_chunk(slot, anchors=()):
        wu_ref, wd_ref = wu_buf.at[slot], wd_buf.at[slot]
        kh = D_MODEL // 2
        for n in range(0, tf, MXU_TILE):
            u = jnp.dot(h_ref[h_mlp, :, 0:kh], wu_ref[0:kh, n:n + MXU_TILE],
                        preferred_element_type=F32)
            yield
            u = u + jnp.dot(h_ref[h_mlp, :, kh:], wu_ref[kh:, n:n + MXU_TILE],
                            preferred_element_type=F32)
            scale = scale_ref[h_mlp]
            u = u * jnp.concatenate([scale] * (MXU_TILE // LANES), axis=1)
            u = jnp.maximum(u, 0.0)
            uu_ref[:, n:n + MXU_TILE] = (u * u).astype(BF16)
            yield
        for m in range(0, D_MODEL, MXU_TILE):
            while anchors:
                uu_ref[0:2 * SUBLANES, 0:LANES] += _exact_zero(anchors.pop()).astype(BF16)
            o_ref[:, m:m + MXU_TILE] += jnp.dot(uu_ref[...], wd_ref[:, m:m + MXU_TILE],
                                                preferred_element_type=F32)
            yield

    t_seq = blk % blocks_per_seq
    parity = t_seq % 2
    mix = mix_ref.at[rows]

    def col(a, w):
        return proj_ref.at[:, a:a + w]

    def attention_streams():
        _mixers_prelude(t_seq, krep_ref, vwin_ref, tail_ref, mask_ref)
        x1_ref[rows, :] = x_ref[...]
        return [(_attention_pieces(parity, col(0, ATTN_WIDTH), col(SPLIT_K, KV_WIDTH),
                                   col(SPLIT_V, KV_WIDTH), gq_ref, gk_ref, sinks_ref, og_ref, mix,
                                   krep_ref, vwin_ref, ones_ref, mask_ref), ATTENTION_PIECES)]

    def conv_sgu_streams(anchors):
        return [
            (_conv_pieces(parity, col(SPLIT_CONV, CONV_WIDTH),
                          col(SPLIT_CONV + CONV_WIDTH, CONV_WIDTH), cb_ref, clg_ref, clb_ref,
                          og_ref, mix, hbuf_ref, tail_ref, hs_ref, w8_ref, anchors), CONV_PIECES),
            (_sgu_pieces(col(SPLIT_SGU, SGU_WIDTH), col(SPLIT_SGU + SGU_WIDTH, SGU_WIDTH),
                         slg_ref, slb_ref, sb_ref, og_ref, mix, wsg_ref), 1)]

    def out_proj_pieces():
        ssq = None
        for n in range(0, D_MODEL, MXU_TILE):
            x1 = x1_ref[:, n:n + MXU_TILE] + jnp.dot(mix_ref[...], wo_ref[:, n:n + MXU_TILE],
                                                     preferred_element_type=F32)
            x1_ref[:, n:n + MXU_TILE] = x1
            part = jnp.sum(x1 * x1, axis=-1, keepdims=True)
            ssq = part if ssq is None else ssq + part
            h_ref[h_new, :, n:n + MXU_TILE] = (x1 * g2_ref[:, n:n + MXU_TILE]).astype(BF16)
            yield
        scale_ref[h_new] = jnp.broadcast_to(lax.rsqrt(ssq * (1.0 / D_MODEL) + EPS),
                                            scale_ref.shape[1:])

    def interleave(streams):
        done = [0] * len(streams)
        live = set(range(len(streams)))
        while live:
            s = min(live, key=lambda j: (done[j] + 1) / streams[j][1])
            done[s] += 1
            if next(streams[s][0], _DONE) is _DONE:
                live.remove(s)

    middle = jnp.logical_not(first | last)
    is_proj = g == n_g - 1

    wait_weights(0)

    @pl.when(middle)
    def _():
        interleave(attention_streams() + [(mlp_chunk(0), MLP_PIECES)])

    @pl.when(first)
    def _():
        interleave(attention_streams())

    @pl.when(last)
    def _():
        interleave([(mlp_chunk(0), MLP_PIECES)])

    @pl.when(jnp.logical_not(final))
    def _():
        start_weights((2 * g + 2) % n_chunks, 0)

    wait_weights(1)

    @pl.when(middle)
    def _():
        anchors = []
        interleave(conv_sgu_streams(anchors) + [(mlp_chunk(1, anchors), MLP_PIECES)])

    @pl.when(first)
    def _():
        interleave(conv_sgu_streams([]))

    @pl.when(last)
    def _():
        interleave([(mlp_chunk(1), MLP_PIECES)])

    @pl.when(is_proj & jnp.logical_not(last))
    def _():
        interleave([(out_proj_pieces(), OUT_PROJ_PIECES)])

    @pl.when(jnp.logical_not(final))
    def _():
        start_weights((2 * g + 3) % n_chunks, 1)


def _layer_tail(proj, x2, B, S, tm, tf, gq, gk, sinks, cw, cb, clg, clb, slg, slb, sw, sb, og,
                wo, g2, wu, wd):
    T = B * S
    n_tiles = T // tm
    bpt = tm // BLOCK
    assert D_FF // tf == 2 * bpt, "each mixer block step runs two MLP chunks"
    assert S % BLOCK == 0 and T % tm == 0

    def blk(i, g):
        return jnp.minimum(i, n_tiles - 1) * bpt + g

    def whole(shape, **kw):
        return pl.BlockSpec(shape, lambda i, g: (0,) * len(shape), **kw)

    return pl.pallas_call(
        functools.partial(_layer_tail_kernel, n_tiles=n_tiles, blocks_per_tile=bpt,
                          blocks_per_seq=S // BLOCK),
        grid=(n_tiles + 1, bpt),
        in_specs=[
            pl.BlockSpec((BLOCK, D_IN), lambda i, g: (blk(i, g), 0)),
            pl.BlockSpec((BLOCK, D_MODEL), lambda i, g: (blk(i, g), 0)),
            whole((1, ATTN_WIDTH)), whole((1, KV_WIDTH)),
            pl.BlockSpec(memory_space=pltpu.MemorySpace.SMEM),
            whole((CONV_KERNEL, CONV_WIDTH)), whole((1, CONV_WIDTH)),
            whole((1, CONV_WIDTH)), whole((1, CONV_WIDTH)),
            whole((1, SGU_WIDTH)), whole((1, SGU_WIDTH)),
            whole((SGU_HEADS // 2, 2 * CHUNK, CHUNK)), whole((CHUNK, SGU_WIDTH)),
            whole((1, D_MODEL)),
            whole((D_MODEL, D_MODEL), pipeline_mode=pl.Buffered(1)),
            whole((1, D_MODEL)),
            pl.BlockSpec(memory_space=pl.ANY),
            pl.BlockSpec(memory_space=pl.ANY),
        ],
        out_specs=pl.BlockSpec((tm, D_MODEL), lambda i, g: (jnp.maximum(i - 1, 0), 0)),
        out_shape=jax.ShapeDtypeStruct((T, D_MODEL), F32),
        scratch_shapes=[
            pltpu.VMEM((2, D_MODEL, tf), BF16),
            pltpu.VMEM((2, tf, D_MODEL), BF16),
            pltpu.SemaphoreType.DMA((2, 2)),
            pltpu.VMEM((tm, D_MODEL), BF16),
            pltpu.VMEM((tm, D_MODEL), F32),
            pltpu.VMEM((2, tm, D_MODEL), BF16),
            pltpu.VMEM((2, tm, LANES), F32),
            pltpu.VMEM((tm, tf), BF16),
            pltpu.VMEM((N_KV_HEADS, 2 * WINDOW, KV_WIDTH), BF16),
            pltpu.VMEM((2 * WINDOW, KV_WIDTH), BF16),
            pltpu.VMEM((CONV_HALO + BLOCK, CONV_WIDTH), F32),
            pltpu.VMEM((2, CONV_HALO, CONV_WIDTH), F32),
            pltpu.VMEM((SUBLANES - 1, CONV_HALO + BLOCK - SUBLANES, CONV_WIDTH), F32),
            pltpu.VMEM((CONV_KERNEL, SUBLANES, CONV_WIDTH), F32),
            pltpu.VMEM((SGU_HEADS // 2, 2 * CHUNK, CHUNK), BF16),
            pltpu.VMEM((KV_WIDTH, KV_WIDTH), BF16),
            pltpu.VMEM((2, WINDOW, 2 * WINDOW), F32),
        ],
        compiler_params=pltpu.CompilerParams(
            dimension_semantics=("arbitrary", "arbitrary"), vmem_limit_bytes=VMEM_LIMIT),
        name="layer_tail",
    )(proj, x2, gq, gk, sinks, cw, cb, clg, clb, slg, slb, sw, sb, og, wo, g2, wu, wd)


def _attn_perm():
    perm = np.empty((ATTN_WIDTH,), np.int32)
    for p in range(N_KV_HEADS // 2):
        for g in range(GROUP):
            for e in range(2):
                head = GROUP * (2 * p + e) + g
                dst = (GROUP * p + g) * LANES + e * HALF
                perm[dst:dst + HEAD_DIM] = np.arange(head * HEAD_DIM, (head + 1) * HEAD_DIM)
    return perm


def _pick_tile(n, pref):
    while n % pref:
        pref //= 2
    return pref


def kernel(x, ln1_g, w_in, q_norm_g, k_norm_g, sinks, conv_w, conv_b, conv_ln_g, conv_ln_b,
           sgu_ln_g, sgu_ln_b, sgu_w, sgu_b, out_norm_g, w_out, ln2_g, w_up, w_down):
    B, S, D = x.shape
    depth = w_in.shape[0]
    assert D == D_MODEL and S % WINDOW == 0
    T = B * S
    tm_proj = _pick_tile(T, 512)
    tm_tail = _pick_tile(T, 512)
    tf = D_FF // (2 * (tm_tail // BLOCK))
    perm = np.concatenate([_attn_perm(), np.arange(ATTN_WIDTH, D_MODEL, dtype=np.int32)])

    x2 = x.reshape(T, D)
    for l in range(depth):
        proj = _in_proj(x2, ln1_g[l][None, :], _cast_rows(w_in, l, 256, name="cast_w_in"), tm_proj)
        x2 = _layer_tail(
            proj, x2, B, S, tm_tail, tf,
            jnp.tile(q_norm_g[l], N_Q_HEADS)[None, :], jnp.tile(k_norm_g[l], N_KV_HEADS)[None, :],
            sinks[l], conv_w[l], conv_b[l][None, :], conv_ln_g[l][None, :], conv_ln_b[l][None, :],
            sgu_ln_g[l][None, :], sgu_ln_b[l][None, :],
            sgu_w[l].reshape(SGU_HEADS // 2, 2 * CHUNK, CHUNK),
            jnp.repeat(sgu_b[l].T, HEAD_DIM, axis=1),
            out_norm_g[l][perm][None, :],
            _cast_w_out(w_out, l), ln2_g[l][None, :],
            _cast_column_chunks(w_up, l, tf), _cast_rows(w_down, l, 512, name="cast_w_down"))
    return x2.reshape(B, S, D)
```
